```python
import jax, jax.numpy as jnp
from jax import lax
import numpy as np

D_MODEL = 1024
BATCH = 4
SEQ = 8192
DEPTH = 2

GRID_W = 64
CTX_LEN = 256
ROPE_THETA = 10000.0
Q_BLOCK = 128
EPS = 1e-6

A_HEADS = 8
A_NOPE = 64
A_ROPE = 32
A_V = 64
A_Q_RANK = 384
A_KV_RANK = 256
A_WIDTH = A_HEADS * A_V
A_SCALE = (A_NOPE + A_ROPE) ** -0.5

B_HEADS = 8
B_KV_HEADS = 2
B_GROUP = B_HEADS // B_KV_HEADS
B_HD = 64
B_WIDTH = B_HEADS * B_HD
B_SCALE = B_HD ** -0.5

EVEN_WIDTH = A_WIDTH + B_WIDTH
EVEN_SPLITS = [A_Q_RANK, A_KV_RANK, A_ROPE, A_WIDTH,
               B_HEADS * B_HD, B_KV_HEADS * B_HD, B_KV_HEADS * B_HD, B_WIDTH]
EVEN_IN = sum(EVEN_SPLITS)

C_CHUNK = 128
C_GROUPS = 8
C_WIDTH = 1024
C_GROUP_W = C_WIDTH // C_GROUPS
ODD_IN = 3 * C_WIDTH

N_EVEN = (DEPTH + 1) // 2
N_ODD = DEPTH // 2
DEEPNORM_ALPHA = (2 * DEPTH) ** 0.25
DEEPNORM_BETA = (8 * DEPTH) ** -0.25

kernel_name = 'hybrid_mla_gqa_gmlp_prefix_dit'


def rms_norm(x, g):
    xf = x.astype(jnp.float32)
    y = xf * lax.rsqrt(jnp.mean(xf * xf, axis=-1, keepdims=True) + EPS)
    return (y * g.astype(jnp.float32)).astype(x.dtype)


def layer_norm(x, g, b):
    xf = x.astype(jnp.float32)
    mu = jnp.mean(xf, axis=-1, keepdims=True)
    var = jnp.mean(jnp.square(xf - mu), axis=-1, keepdims=True)
    y = (xf - mu) * lax.rsqrt(var + EPS)
    return (y * g.astype(jnp.float32) + b.astype(jnp.float32)).astype(x.dtype)


def modulation(cond, w_mod, b_mod):
    m = jax.nn.silu(cond) @ w_mod + b_mod
    return jnp.split(m, 3, axis=-1)


def modulate(x, shift, scale):
    return x * (1.0 + scale) + shift


def axial_rope_angles(n_tok, d_rot):
    rows = n_tok // GRID_W
    row = jnp.repeat(jnp.arange(rows, dtype=jnp.float32), GRID_W)
    col = jnp.tile(jnp.arange(GRID_W, dtype=jnp.float32), rows)
    d_axis = d_rot // 2
    inv_freq = ROPE_THETA ** (-jnp.arange(0, d_axis, 2, dtype=jnp.float32) / d_axis)
    ang = jnp.stack([row[:, None] * inv_freq, col[:, None] * inv_freq], axis=1)
    return (jnp.cos(ang), jnp.sin(ang))


def apply_axial_rope(x, cos, sin):
    b, s, h, d = x.shape
    xa = x.reshape(b, s, h, 2, d // 2)
    x1, x2 = jnp.split(xa, 2, axis=-1)
    cs = cos[None, :, None].astype(x.dtype)
    sn = sin[None, :, None].astype(x.dtype)
    out = jnp.concatenate([x1 * cs - x2 * sn, x2 * cs + x1 * sn], axis=-1)
    return out.reshape(b, s, h, d)


def block_attention(q, k, v, scale):
    b, s, hk, g, dk = q.shape
    nb = s // Q_BLOCK
    qb = jnp.moveaxis(q.reshape(b, nb, Q_BLOCK, hk, g, dk), 1, 0)

    def one_block(qi):
        sc = jnp.einsum('bqhgd,bthd->bhgqt', qi, k).astype(jnp.float32) * scale
        p = jax.nn.softmax(sc, axis=-1).astype(v.dtype)
        return jnp.einsum('bhgqt,bthe->bqhge', p, v)

    out = lax.map(one_block, qb)
    return jnp.moveaxis(out, 0, 1).reshape(b, s, hk, g, v.shape[-1])


def even_heads(h, a_q_norm, a_kv_norm, a_w_uq, a_w_ukv, b_q_norm, b_k_norm, rope):
    bsz, n, _ = h.shape
    idx = [int(i) for i in np.cumsum(EVEN_SPLITS)[:-1]]
    cq, ckv, kr, g_a, qb, kb, vb, g_b = jnp.split(h, idx, axis=-1)
    q_a = (rms_norm(cq, a_q_norm) @ a_w_uq).reshape(bsz, n, A_HEADS, A_NOPE + A_ROPE)
    kv_a = (rms_norm(ckv, a_kv_norm) @ a_w_ukv).reshape(bsz, n, A_HEADS, A_NOPE + A_V)
    q_nope, q_pe = q_a[..., :A_NOPE], q_a[..., A_NOPE:]
    k_nope, v_a = kv_a[..., :A_NOPE], kv_a[..., A_NOPE:]
    k_pe = kr.reshape(bsz, n, 1, A_ROPE)
    q_b = rms_norm(qb.reshape(bsz, n, B_HEADS, B_HD), b_q_norm)
    k_b = rms_norm(kb.reshape(bsz, n, B_KV_HEADS, B_HD), b_k_norm)
    v_b = vb.reshape(bsz, n, B_KV_HEADS, B_HD)
    if rope is not None:
        cos_a, sin_a, cos_b, sin_b = rope
        q_pe = apply_axial_rope(q_pe, cos_a, sin_a)
        k_pe = apply_axial_rope(k_pe, cos_a, sin_a)
        q_b = apply_axial_rope(q_b, cos_b, sin_b)
        k_b = apply_axial_rope(k_b, cos_b, sin_b)
    q_a = jnp.concatenate([q_nope, q_pe], axis=-1)
    k_a = jnp.concatenate([k_nope, jnp.broadcast_to(k_pe, (bsz, n, A_HEADS, A_ROPE))], axis=-1)
    return (q_a, k_a, v_a, g_a, q_b, k_b, v_b, g_b)


def attend_and_merge(q_a, k_a, v_a, q_b, k_b, v_b, g_a, g_b, w_out):
    bsz, n = q_a.shape[:2]
    o_a = block_attention(q_a[:, :, :, None], k_a, v_a, A_SCALE).reshape(bsz, n, A_WIDTH)
    o_b = block_attention(q_b.reshape(bsz, n, B_KV_HEADS, B_GROUP, B_HD), k_b, v_b,
                          B_SCALE).reshape(bsz, n, B_WIDTH)
    merged = jnp.concatenate([o_a * jax.nn.silu(g_a), o_b * jax.nn.silu(g_b)], axis=-1)
    return merged @ w_out


def even_layer(x, xc, c, c_ctx, w_mod, b_mod, w_in, a_q_norm, a_kv_norm, a_w_uq, a_w_ukv,
               b_q_norm, b_k_norm, w_out, ln_g, ln_b, update_ctx):
    n = x.shape[1]
    rope = axial_rope_angles(n, A_ROPE) + axial_rope_angles(n, B_HD)
    shift, scale, gate = modulation(c[:, None, :], w_mod, b_mod)
    shift_c, scale_c, gate_c = modulation(c_ctx[None, :], w_mod, b_mod)
    q_a, k_a, v_a, g_a, q_b, k_b, v_b, g_b = even_heads(
        modulate(x, shift, scale) @ w_in, a_q_norm, a_kv_norm, a_w_uq, a_w_ukv, b_q_norm, b_k_norm, rope)
    q_ac, k_ac, v_ac, g_ac, q_bc, k_bc, v_bc, g_bc = even_heads(
        modulate(xc, shift_c, scale_c) @ w_in, a_q_norm, a_kv_norm, a_w_uq, a_w_ukv, b_q_norm, b_k_norm, None)
    y = attend_and_merge(q_a, jnp.concatenate([k_ac, k_a], axis=1), jnp.concatenate([v_ac, v_a], axis=1),
                         q_b, jnp.concatenate([k_bc, k_b], axis=1), jnp.concatenate([v_bc, v_b], axis=1),
                         g_a, g_b, w_out)
    x_new = layer_norm(DEEPNORM_ALPHA * x + gate * y, ln_g, ln_b)
    if update_ctx:
        yc = attend_and_merge(q_ac, k_ac, v_ac, q_bc, k_bc, v_bc, g_ac, g_bc, w_out)
        xc = layer_norm(DEEPNORM_ALPHA * xc + gate_c * yc, ln_g, ln_b)
    return x_new, xc


def chunk_gmlp(h, v_ln_g, v_ln_b, w_s, b_s):
    bsz, n, _ = h.shape
    u, v, g = jnp.split(h, 3, axis=-1)
    u = jax.nn.gelu(u)
    v = layer_norm(jax.nn.gelu(v), v_ln_g, v_ln_b)
    nc = n // C_CHUNK
    vv = v.reshape(bsz, nc, C_CHUNK, C_GROUPS, C_GROUP_W)
    mixed = jnp.einsum('gpq,bnqgc->bnpgc', w_s, vv) + b_s.T[None, None, :, :, None]
    return u * mixed.reshape(bsz, n, C_WIDTH) * jax.nn.silu(g)


def odd_layer(x, xc, c, c_ctx, w_mod, b_mod, w_in, v_ln_g, v_ln_b, w_s, b_s, w_out,
              ln_g, ln_b, update_ctx):
    shift, scale, gate = modulation(c[:, None, :], w_mod, b_mod)
    y = chunk_gmlp(modulate(x, shift, scale) @ w_in, v_ln_g, v_ln_b, w_s, b_s) @ w_out
    x_new = layer_norm(DEEPNORM_ALPHA * x + gate * y, ln_g, ln_b)
    if update_ctx:
        shift_c, scale_c, gate_c = modulation(c_ctx[None, :], w_mod, b_mod)
        yc = chunk_gmlp(modulate(xc, shift_c, scale_c) @ w_in, v_ln_g, v_ln_b, w_s, b_s) @ w_out
        xc = layer_norm(DEEPNORM_ALPHA * xc + gate_c * yc, ln_g, ln_b)
    return x_new, xc


def setup_inputs(seed: int = 0) -> dict:
    key = jax.random.key(seed)
    ks = iter(jax.random.split(key, 32))
    D = D_MODEL

    def nrm(shape, scale):
        return jax.random.normal(next(ks), shape, jnp.float32) * scale

    def gain(shape):
        return 1.0 + nrm(shape, 0.05)

    return {
        'x': nrm((BATCH, SEQ, D), 1.0),
        'c': nrm((BATCH, D), 1.0),
        'ctx': nrm((BATCH, CTX_LEN, D), 1.0),
        'c_ctx': nrm((D,), 1.0),
        'e_w_mod': nrm((N_EVEN, D, 3 * D), D ** -0.5),
        'e_b_mod': nrm((N_EVEN, 3 * D), 0.02),
        'e_w_in': nrm((N_EVEN, D, EVEN_IN), D ** -0.5),
        'e_a_q_norm': gain((N_EVEN, A_Q_RANK)),
        'e_a_kv_norm': gain((N_EVEN, A_KV_RANK)),
        'e_a_w_uq': nrm((N_EVEN, A_Q_RANK, A_HEADS * (A_NOPE + A_ROPE)), A_Q_RANK ** -0.5),
        'e_a_w_ukv': nrm((N_EVEN, A_KV_RANK, A_HEADS * (A_NOPE + A_V)), A_KV_RANK ** -0.5),
        'e_b_q_norm': gain((N_EVEN, B_HD)),
        'e_b_k_norm': gain((N_EVEN, B_HD)),
        'e_w_out': nrm((N_EVEN, EVEN_WIDTH, D), EVEN_WIDTH ** -0.5 * DEEPNORM_BETA),
        'e_ln_g': gain((N_EVEN, D)),
        'e_ln_b': nrm((N_EVEN, D), 0.02),
        'o_w_mod': nrm((N_ODD, D, 3 * D), D ** -0.5),
        'o_b_mod': nrm((N_ODD, 3 * D), 0.02),
        'o_w_in': nrm((N_ODD, D, ODD_IN), D ** -0.5),
        'o_v_ln_g': gain((N_ODD, C_WIDTH)),
        'o_v_ln_b': nrm((N_ODD, C_WIDTH), 0.02),
        'o_w_s': nrm((N_ODD, C_GROUPS, C_CHUNK, C_CHUNK), C_CHUNK ** -0.5),
        'o_b_s': 1.0 + nrm((N_ODD, C_GROUPS, C_CHUNK), 0.02),
        'o_w_out': nrm((N_ODD, C_WIDTH, D), C_WIDTH ** -0.5 * DEEPNORM_BETA),
        'o_ln_g': gain((N_ODD, D)),
        'o_ln_b': nrm((N_ODD, D), 0.02),
    }


def reference(x, c, ctx, c_ctx, e_w_mod, e_b_mod, e_w_in, e_a_q_norm, e_a_kv_norm, e_a_w_uq,
              e_a_w_ukv, e_b_q_norm, e_b_k_norm, e_w_out, e_ln_g, e_ln_b, o_w_mod, o_b_mod,
              o_w_in, o_v_ln_g, o_v_ln_b, o_w_s, o_b_s, o_w_out, o_ln_g, o_ln_b):
    xl, xc = x, ctx
    for i in range(DEPTH):
        update_ctx = any(j % 2 == 0 for j in range(i + 1, DEPTH))
        k = i // 2
        if i % 2 == 0:
            xl, xc = even_layer(xl, xc, c, c_ctx, e_w_mod[k], e_b_mod[k], e_w_in[k], e_a_q_norm[k],
                                e_a_kv_norm[k], e_a_w_uq[k], e_a_w_ukv[k], e_b_q_norm[k], e_b_k_norm[k],
                                e_w_out[k], e_ln_g[k], e_ln_b[k], update_ctx)
        else:
            xl, xc = odd_layer(xl, xc, c, c_ctx, o_w_mod[k], o_b_mod[k], o_w_in[k], o_v_ln_g[k],
                               o_v_ln_b[k], o_w_s[k], o_b_s[k], o_w_out[k], o_ln_g[k], o_ln_b[k],
                               update_ctx)
    return xl
```

```python
import functools

import jax
import jax.numpy as jnp
from jax import lax
from jax.experimental import pallas as pl
from jax.experimental.pallas import tpu as pltpu

D_MODEL = 1024
BATCH = 4
SEQ = 8192
DEPTH = 2
GRID_W = 64
CTX_LEN = 256
ROPE_THETA = 10000.0
EPS = 1e-6

A_HEADS = 8
A_NOPE = 64
A_ROPE = 32
A_V = 64
A_Q_RANK = 384
A_KV_RANK = 256
A_WIDTH = A_HEADS * A_V
A_SCALE = (A_NOPE + A_ROPE) ** -0.5

B_HEADS = 8
B_KV_HEADS = 2
B_GROUP = B_HEADS // B_KV_HEADS
B_HD = 64
B_WIDTH = B_HEADS * B_HD
B_SCALE = B_HD ** -0.5

EVEN_SPLITS = [A_Q_RANK, A_KV_RANK, A_ROPE, A_WIDTH,
               B_HEADS * B_HD, B_KV_HEADS * B_HD, B_KV_HEADS * B_HD, B_WIDTH]
EVEN_IN = sum(EVEN_SPLITS)

C_CHUNK = 128
C_GROUPS = 8
C_WIDTH = 1024
C_GROUP_W = C_WIDTH // C_GROUPS
ODD_IN = 3 * C_WIDTH

DEEPNORM_ALPHA = (2 * DEPTH) ** 0.25

T_ALL = CTX_LEN + SEQ
HEAD_PAD = 128
MOD_ROWS = 8

TOK_TILE = 256
N_TOK_TILES = T_ALL // TOK_TILE
Q_TILE = 512
POST_TILE = 256
MOD_COL_TILE = 768
NEG_BIG = -1e30
VMEM_LIMIT = 56 * 1024 * 1024

F32 = jnp.float32
BF16 = jnp.bfloat16


def _layer_norm_rows(xf, g, b):
    mu = jnp.mean(xf, axis=-1, keepdims=True)
    var = jnp.mean(jnp.square(xf - mu), axis=-1, keepdims=True)
    return (xf - mu) * lax.rsqrt(var + EPS) * g + b


def _mod_kernel(c_ref, w_ref, b_ref, o_ref):
    a = jax.nn.silu(c_ref[...])
    o_ref[...] = jnp.dot(a, w_ref[...], preferred_element_type=F32,
                         precision=lax.Precision.HIGHEST) + b_ref[...]


def _modulation(cond, w_mod, b_mod):
    n = w_mod.shape[1]
    return pl.pallas_call(
        _mod_kernel,
        grid=(n // MOD_COL_TILE,),
        in_specs=[pl.BlockSpec((MOD_ROWS, D_MODEL), lambda j: (0, 0)),
                  pl.BlockSpec((D_MODEL, MOD_COL_TILE), lambda j: (0, j)),
                  pl.BlockSpec((1, MOD_COL_TILE), lambda j: (0, j))],
        out_specs=pl.BlockSpec((MOD_ROWS, MOD_COL_TILE), lambda j: (0, j)),
        out_shape=jax.ShapeDtypeStruct((MOD_ROWS, n), F32),
        name="modulation",
    )(cond, w_mod, b_mod.reshape(1, n))


def _rms_cols(v, g):
    ms = jnp.mean(v * v, axis=0, keepdims=True)
    return v * lax.rsqrt(ms + EPS) * g


def _rope_cols(p, tab):
    q = p.shape[0] // 4
    x1r, x2r, x1c, x2c = p[0:q], p[q:2 * q], p[2 * q:3 * q], p[3 * q:4 * q]
    cr, sr, cc, sc = tab[0:q], tab[q:2 * q], tab[2 * q:3 * q], tab[3 * q:4 * q]
    return jnp.concatenate([x1r * cr - x2r * sr, x2r * cr + x1r * sr,
                            x1c * cc - x2c * sc, x2c * cc + x1c * sc], axis=0)


def _even_proj_kernel(x_ref, ctx_ref, mod_ref, winT_ref, gq_ref, gkv_ref, wuqT_ref, wukT_ref,
                      wuvT_ref, gbq_ref, gbk_ref, tab_a_ref, tab_b_ref,
                      qa_ref, ka_ref, va_ref, qb_ref, kb_ref, vb_ref, g_ref):
    b = pl.program_id(0)
    t = pl.program_id(1)
    is_ctx = t == 0
    xin = jnp.where(is_ctx, ctx_ref[0], x_ref[0])
    mod = mod_ref[pl.ds(jnp.where(is_ctx, BATCH, b), 1), :]
    shift, scale = mod[:, :D_MODEL], mod[:, D_MODEL:2 * D_MODEL]
    xm = (xin * (1.0 + scale) + shift).astype(BF16)
    hT = lax.dot_general(winT_ref[...], xm, (((1,), (1,)), ((), ())), preferred_element_type=F32)
    o = 0
    parts = []
    for w in EVEN_SPLITS:
        parts.append(hT[o:o + w])
        o += w
    cq, ckv, kr, ga, qb, kb, vb, gb = parts
    tt = xin.shape[0]
    zpad_a = jnp.zeros((HEAD_PAD - A_NOPE - A_ROPE, tt), F32)
    zpad_b = jnp.zeros((HEAD_PAD - B_HD, tt), F32)
    tab_a = tab_a_ref[...]
    tab_b = tab_b_ref[...]

    cqn = _rms_cols(cq, gq_ref[...]).astype(BF16)
    ckvn = _rms_cols(ckv, gkv_ref[...]).astype(BF16)
    qa = jnp.dot(wuqT_ref[...], cqn, preferred_element_type=F32)
    kn = jnp.dot(wukT_ref[...], ckvn, preferred_element_type=F32)
    va = jnp.dot(wuvT_ref[...], ckvn, preferred_element_type=F32)
    k_pe = _rope_cols(kr, tab_a)
    dq = A_NOPE + A_ROPE
    qa_heads, ka_heads = [], []
    for h in range(A_HEADS):
        qa_heads += [qa[dq * h:dq * h + A_NOPE],
                     _rope_cols(qa[dq * h + A_NOPE:dq * (h + 1)], tab_a), zpad_a]
        ka_heads += [kn[A_NOPE * h:A_NOPE * (h + 1)], k_pe, zpad_a]
    ka_ref[0] = jnp.concatenate(ka_heads, axis=0).T.astype(BF16)
    va_ref[0, 0] = va.astype(BF16)

    gbk = gbk_ref[...]
    kb_heads = []
    for j in range(B_KV_HEADS):
        kb_heads += [_rope_cols(_rms_cols(kb[B_HD * j:B_HD * (j + 1)], gbk), tab_b), zpad_b]
    kb_ref[0] = jnp.concatenate(kb_heads, axis=0).T.astype(BF16)
    vb_ref[0, 0] = vb.astype(BF16)

    @pl.when(t > 0)
    def _():
        qa_ref[0] = (jnp.concatenate(qa_heads, axis=0) * A_SCALE).astype(BF16)
        gbq = gbq_ref[...]
        qb_heads = []
        for h in range(B_HEADS):
            qb_heads += [_rope_cols(_rms_cols(qb[B_HD * h:B_HD * (h + 1)], gbq), tab_b), zpad_b]
        qb_ref[0] = (jnp.concatenate(qb_heads, axis=0) * B_SCALE).astype(BF16)
        g_ref[0] = jnp.concatenate([jax.nn.silu(ga), jax.nn.silu(gb)], axis=0).astype(BF16)


def _even_projection(x, ctx, mod, winT, gq, gkv, wuqT, wukT, wuvT, gbq, gbk, tab_a, tab_b):
    tt = TOK_TILE
    nt = N_TOK_TILES
    lat = lambda b, t: (b, 0, jnp.maximum(t - 1, 0))
    full2 = lambda b, t: (0, 0)
    out_shape = (
        jax.ShapeDtypeStruct((BATCH, A_HEADS * HEAD_PAD, SEQ), BF16),
        jax.ShapeDtypeStruct((BATCH, T_ALL, A_HEADS * HEAD_PAD), BF16),
        jax.ShapeDtypeStruct((BATCH, nt, A_WIDTH, tt), BF16),
        jax.ShapeDtypeStruct((BATCH, B_HEADS * HEAD_PAD, SEQ), BF16),
        jax.ShapeDtypeStruct((BATCH, T_ALL, B_KV_HEADS * HEAD_PAD), BF16),
        jax.ShapeDtypeStruct((BATCH, nt, B_KV_HEADS * B_HD, tt), BF16),
        jax.ShapeDtypeStruct((BATCH, A_WIDTH + B_WIDTH, SEQ), BF16),
    )
    out_specs = (
        pl.BlockSpec((1, A_HEADS * HEAD_PAD, tt), lat),
        pl.BlockSpec((1, tt, A_HEADS * HEAD_PAD), lambda b, t: (b, t, 0)),
        pl.BlockSpec((1, 1, A_WIDTH, tt), lambda b, t: (b, t, 0, 0)),
        pl.BlockSpec((1, B_HEADS * HEAD_PAD, tt), lat),
        pl.BlockSpec((1, tt, B_KV_HEADS * HEAD_PAD), lambda b, t: (b, t, 0)),
        pl.BlockSpec((1, 1, B_KV_HEADS * B_HD, tt), lambda b, t: (b, t, 0, 0)),
        pl.BlockSpec((1, A_WIDTH + B_WIDTH, tt), lat),
    )
    in_specs = [
        pl.BlockSpec((1, tt, D_MODEL), lambda b, t: (b, jnp.maximum(t - 1, 0), 0)),
        pl.BlockSpec((1, CTX_LEN, D_MODEL), lambda b, t: (b, 0, 0)),
        pl.BlockSpec(mod.shape, full2),
        pl.BlockSpec(winT.shape, full2),
        pl.BlockSpec(gq.shape, full2),
        pl.BlockSpec(gkv.shape, full2),
        pl.BlockSpec(wuqT.shape, full2),
        pl.BlockSpec(wukT.shape, full2),
        pl.BlockSpec(wuvT.shape, full2),
        pl.BlockSpec(gbq.shape, full2),
        pl.BlockSpec(gbk.shape, full2),
        pl.BlockSpec((A_ROPE, tt), lambda b, t: (0, t)),
        pl.BlockSpec((B_HD, tt), lambda b, t: (0, t)),
    ]
    return pl.pallas_call(
        _even_proj_kernel,
        grid=(BATCH, nt),
        in_specs=in_specs,
        out_specs=out_specs,
        out_shape=out_shape,
        compiler_params=pltpu.CompilerParams(
            dimension_semantics=("arbitrary", "arbitrary"), vmem_limit_bytes=VMEM_LIMIT),
        name="even_projection",
    )(x, ctx, mod, winT, gq, gkv, wuqT, wukT, wuvT, gbq, gbk, tab_a, tab_b)


def _attn_kernel(q_ref, k_ref, v_ref, g_ref, o_ref):
    q = q_ref[0]
    tq = q.shape[1]
    dv = v_ref.shape[2]

    def body(j, carry):
        m, l, acc = carry
        k = k_ref[0, pl.ds(pl.multiple_of(j * TOK_TILE, TOK_TILE), TOK_TILE), :]
        s = jnp.dot(k, q, preferred_element_type=F32)
        m_new = jnp.maximum(m, jnp.max(s, axis=0, keepdims=True))
        alpha = jnp.exp(m - m_new)
        p = jnp.exp(s - m_new)
        l = alpha * l + jnp.sum(p, axis=0, keepdims=True)
        acc = alpha * acc + jnp.dot(v_ref[0, j], p.astype(BF16), preferred_element_type=F32)
        return m_new, l, acc

    init = (jnp.full((1, tq), NEG_BIG, F32), jnp.zeros((1, tq), F32), jnp.zeros((dv, tq), F32))
    _, l, acc = lax.fori_loop(0, N_TOK_TILES, body, init)
    o_ref[0] = (acc / l * g_ref[0].astype(F32)).astype(BF16)


def _attention(qT, k, vT, gT, *, group, gate_row0, name):
    n_heads = qT.shape[1] // HEAD_PAD
    dv = vT.shape[2] // (n_heads // group)
    g_blk0 = gate_row0 // dv
    return pl.pallas_call(
        _attn_kernel,
        grid=(BATCH, n_heads, SEQ // Q_TILE),
        in_specs=[
            pl.BlockSpec((1, HEAD_PAD, Q_TILE), lambda b, h, i: (b, h, i)),
            pl.BlockSpec((1, T_ALL, HEAD_PAD), lambda b, h, i: (b, 0, h // group)),
            pl.BlockSpec((1, N_TOK_TILES, dv, TOK_TILE), lambda b, h, i: (b, 0, h // group, 0)),
            pl.BlockSpec((1, dv, Q_TILE), lambda b, h, i: (b, h + g_blk0, i)),
        ],
        out_specs=pl.BlockSpec((1, dv, Q_TILE), lambda b, h, i: (b, h, i)),
        out_shape=jax.ShapeDtypeStruct((BATCH, n_heads * dv, SEQ), BF16),
        compiler_params=pltpu.CompilerParams(
            dimension_semantics=("arbitrary", "arbitrary", "arbitrary"),
            vmem_limit_bytes=VMEM_LIMIT),
        name=name,
    )(qT, k, vT, gT)


def _post_kernel(ma_ref, mb_ref, x_ref, mod_e_ref, mod_o_ref, woe_ref, lnge_ref, lnbe_ref,
                 wino_ref, vlng_ref, vlnb_ref, ws_ref, bs_ref, woo_ref, lngo_ref, lnbo_ref,
                 out_ref, z_ref):
    b = pl.program_id(0)
    tn = (((0,), (0,)), ((), ()))
    y = (lax.dot_general(ma_ref[0], woe_ref[0:A_WIDTH, :], tn, preferred_element_type=F32)
         + lax.dot_general(mb_ref[0], woe_ref[A_WIDTH:, :], tn, preferred_element_type=F32))
    gate_e = mod_e_ref[pl.ds(b, 1), :][:, 2 * D_MODEL:]
    x1 = _layer_norm_rows(DEEPNORM_ALPHA * x_ref[0] + gate_e * y, lnge_ref[...], lnbe_ref[...])

    mo = mod_o_ref[pl.ds(b, 1), :]
    shift, scale, gate = mo[:, :D_MODEL], mo[:, D_MODEL:2 * D_MODEL], mo[:, 2 * D_MODEL:]
    xm = (x1 * (1.0 + scale) + shift).astype(BF16)
    h = jnp.dot(xm, wino_ref[...], preferred_element_type=F32)
    u = jax.nn.gelu(h[:, :C_WIDTH])
    v = _layer_norm_rows(jax.nn.gelu(h[:, C_WIDTH:2 * C_WIDTH]), vlng_ref[...], vlnb_ref[...])
    v = v.astype(BF16)
    ug = u * jax.nn.silu(h[:, 2 * C_WIDTH:])
    for n in range(x1.shape[0] // C_CHUNK):
        rows = slice(n * C_CHUNK, (n + 1) * C_CHUNK)
        for gi in range(C_GROUPS):
            cols = slice(gi * C_GROUP_W, (gi + 1) * C_GROUP_W)
            mixed = jnp.dot(ws_ref[gi], v[rows, cols], preferred_element_type=F32) + bs_ref[gi]
            z_ref[rows, cols] = (ug[rows, cols] * mixed).astype(BF16)
    y2 = jnp.dot(z_ref[...], woo_ref[...], preferred_element_type=F32)
    out_ref[0] = _layer_norm_rows(DEEPNORM_ALPHA * x1 + gate * y2, lngo_ref[...], lnbo_ref[...])


def _post(ma, mb, x, mod_e, mod_o, woe, lnge, lnbe, wino, vlng, vlnb, ws, bs, woo, lngo, lnbo):
    tt = POST_TILE
    full2 = lambda b, t: (0, 0)
    full3 = lambda b, t: (0, 0, 0)
    in_specs = [
        pl.BlockSpec((1, A_WIDTH, tt), lambda b, t: (b, 0, t)),
        pl.BlockSpec((1, B_WIDTH, tt), lambda b, t: (b, 0, t)),
        pl.BlockSpec((1, tt, D_MODEL), lambda b, t: (b, t, 0)),
        pl.BlockSpec(mod_e.shape, full2),
        pl.BlockSpec(mod_o.shape, full2),
        pl.BlockSpec(woe.shape, full2),
        pl.BlockSpec(lnge.shape, full2),
        pl.BlockSpec(lnbe.shape, full2),
        pl.BlockSpec(wino.shape, full2),
        pl.BlockSpec(vlng.shape, full2),
        pl.BlockSpec(vlnb.shape, full2),
        pl.BlockSpec(ws.shape, full3),
        pl.BlockSpec(bs.shape, full3),
        pl.BlockSpec(woo.shape, full2),
        pl.BlockSpec(lngo.shape, full2),
        pl.BlockSpec(lnbo.shape, full2),
    ]
    return pl.pallas_call(
        _post_kernel,
        grid=(BATCH, SEQ // tt),
        in_specs=in_specs,
        out_specs=pl.BlockSpec((1, tt, D_MODEL), lambda b, t: (b, t, 0)),
        out_shape=jax.ShapeDtypeStruct((BATCH, SEQ, D_MODEL), F32),
        scratch_shapes=[pltpu.VMEM((tt, C_WIDTH), BF16)],
        compiler_params=pltpu.CompilerParams(
            dimension_semantics=("arbitrary", "arbitrary"), vmem_limit_bytes=VMEM_LIMIT),
        name="out_proj_odd_layer",
    )(ma, mb, x, mod_e, mod_o, woe, lnge, lnbe, wino, vlng, vlnb, ws, bs, woo, lngo, lnbo)


def _rope_table(d_rot):
    rows = SEQ // GRID_W
    row = jnp.repeat(jnp.arange(rows, dtype=F32), GRID_W)
    col = jnp.tile(jnp.arange(GRID_W, dtype=F32), rows)
    d_axis = d_rot // 2
    inv_freq = ROPE_THETA ** (-jnp.arange(0, d_axis, 2, dtype=F32) / d_axis)
    ang_r = row[None, :] * inv_freq[:, None]
    ang_c = col[None, :] * inv_freq[:, None]
    lat = jnp.concatenate([jnp.cos(ang_r), jnp.sin(ang_r), jnp.cos(ang_c), jnp.sin(ang_c)], axis=0)
    one = jnp.ones((d_axis // 2, CTX_LEN), F32)
    zero = jnp.zeros((d_axis // 2, CTX_LEN), F32)
    return jnp.concatenate([jnp.concatenate([one, zero, one, zero], axis=0), lat], axis=1)


def kernel(x, c, ctx, c_ctx, e_w_mod, e_b_mod, e_w_in, e_a_q_norm, e_a_kv_norm, e_a_w_uq, e_a_w_ukv, e_b_q_norm, e_b_k_norm, e_w_out, e_ln_g, e_ln_b, o_w_mod, o_b_mod, o_w_in, o_v_ln_g, o_v_ln_b, o_w_s, o_b_s, o_w_out, o_ln_g, o_ln_b):
    cond = jnp.concatenate(
        [c, c_ctx[None, :], jnp.zeros((MOD_ROWS - BATCH - 1, D_MODEL), F32)], axis=0)
    mod_e = _modulation(cond, e_w_mod[0], e_b_mod[0])
    mod_o = _modulation(cond, o_w_mod[0], o_b_mod[0])

    winT = e_w_in[0].T.astype(BF16)
    wuqT = e_a_w_uq[0].T.astype(BF16)
    wukv = e_a_w_ukv[0].reshape(A_KV_RANK, A_HEADS, A_NOPE + A_V)
    wukT = wukv[:, :, :A_NOPE].reshape(A_KV_RANK, A_HEADS * A_NOPE).T.astype(BF16)
    wuvT = wukv[:, :, A_NOPE:].reshape(A_KV_RANK, A_HEADS * A_V).T.astype(BF16)
    col = lambda v: v.reshape(-1, 1)
    row = lambda v: v.reshape(1, -1)

    qa, ka, va, qb, kb, vb, gates = _even_projection(
        x, ctx, mod_e, winT, col(e_a_q_norm[0]), col(e_a_kv_norm[0]), wuqT, wukT, wuvT,
        col(e_b_q_norm[0]), col(e_b_k_norm[0]), _rope_table(A_ROPE), _rope_table(B_HD))

    ma = _attention(qa, ka, va, gates, group=1, gate_row0=0, name="attention_mla")
    mb = _attention(qb, kb, vb, gates, group=B_GROUP, gate_row0=A_WIDTH, name="attention_gqa")

    bs = jnp.broadcast_to(o_b_s[0][:, :, None], (C_GROUPS, C_CHUNK, C_GROUP_W))
    return _post(ma, mb, x, mod_e, mod_o, e_w_out[0].astype(BF16), row(e_ln_g[0]), row(e_ln_b[0]),
                 o_w_in[0].astype(BF16), row(o_v_ln_g[0]), row(o_v_ln_b[0]),
                 o_w_s[0].astype(BF16), bs, o_w_out[0].astype(BF16), row(o_ln_g[0]), row(o_ln_b[0]))
```

```python
import functools

import jax
import jax.numpy as jnp
from jax import lax
from jax.experimental import pallas as pl
from jax.experimental.pallas import tpu as pltpu

D_MODEL = 1024
BATCH = 4
SEQ = 8192
DEPTH = 2
GRID_W = 64
CTX_LEN = 256
ROPE_THETA = 10000.0
EPS = 1e-6

A_HEADS = 8
A_NOPE = 64
A_ROPE = 32
A_V = 64
A_Q_RANK = 384
A_KV_RANK = 256
A_WIDTH = A_HEADS * A_V
A_SCALE = (A_NOPE + A_ROPE) ** -0.5

B_HEADS = 8
B_KV_HEADS = 2
B_GROUP = B_HEADS // B_KV_HEADS
B_HD = 64
B_WIDTH = B_HEADS * B_HD
B_SCALE = B_HD ** -0.5

EVEN_SPLITS = [A_Q_RANK, A_KV_RANK, A_ROPE, A_WIDTH,
               B_HEADS * B_HD, B_KV_HEADS * B_HD, B_KV_HEADS * B_HD, B_WIDTH]
EVEN_IN = sum(EVEN_SPLITS)

C_CHUNK = 128
C_GROUPS = 8
C_WIDTH = 1024
C_GROUP_W = C_WIDTH // C_GROUPS
ODD_IN = 3 * C_WIDTH

DEEPNORM_ALPHA = (2 * DEPTH) ** 0.25

T_ALL = CTX_LEN + SEQ
HEAD_PAD = 128
MOD_ROWS = 8

TOK_TILE = 256
N_TOK_TILES = T_ALL // TOK_TILE
Q_TILE = 512
ATTN_UNROLL = 32
V_ROWS = 80
LOG2E = 1.4426950408889634
POST_TILE = 256
MOD_COL_TILE = 768
NEG_BIG = -1e30
VMEM_LIMIT = 56 * 1024 * 1024

F32 = jnp.float32
BF16 = jnp.bfloat16


def _layer_norm_rows(xf, g, b):
    mu = jnp.mean(xf, axis=-1, keepdims=True)
    var = jnp.mean(jnp.square(xf - mu), axis=-1, keepdims=True)
    return (xf - mu) * lax.rsqrt(var + EPS) * g + b


def _mod_kernel(c_ref, w_ref, b_ref, o_ref):
    a = jax.nn.silu(c_ref[...])
    o_ref[...] = jnp.dot(a, w_ref[...], preferred_element_type=F32,
                         precision=lax.Precision.HIGHEST) + b_ref[...]


def _modulation(cond, w_mod, b_mod):
    n = w_mod.shape[1]
    return pl.pallas_call(
        _mod_kernel,
        grid=(n // MOD_COL_TILE,),
        in_specs=[pl.BlockSpec((MOD_ROWS, D_MODEL), lambda j: (0, 0)),
                  pl.BlockSpec((D_MODEL, MOD_COL_TILE), lambda j: (0, j)),
                  pl.BlockSpec((1, MOD_COL_TILE), lambda j: (0, j))],
        out_specs=pl.BlockSpec((MOD_ROWS, MOD_COL_TILE), lambda j: (0, j)),
        out_shape=jax.ShapeDtypeStruct((MOD_ROWS, n), F32),
        name="modulation",
    )(cond, w_mod, b_mod.reshape(1, n))


def _rms_cols(v, g):
    ms = jnp.mean(v * v, axis=0, keepdims=True)
    return v * lax.rsqrt(ms + EPS) * g


def _rope_cols(p, tab):
    q = p.shape[0] // 4
    x1r, x2r, x1c, x2c = p[0:q], p[q:2 * q], p[2 * q:3 * q], p[3 * q:4 * q]
    cr, sr, cc, sc = tab[0:q], tab[q:2 * q], tab[2 * q:3 * q], tab[3 * q:4 * q]
    return jnp.concatenate([x1r * cr - x2r * sr, x2r * cr + x1r * sr,
                            x1c * cc - x2c * sc, x2c * cc + x1c * sc], axis=0)


def _even_proj_kernel(x_ref, ctx_ref, mod_ref, winT_ref, gq_ref, gkv_ref, wuqT_ref, wukT_ref,
                      wuvT_ref, gbq_ref, gbk_ref, tab_a_ref, tab_b_ref,
                      qa_ref, ka_ref, va_ref, qb_ref, kb_ref, vb_ref, g_ref):
    b = pl.program_id(0)
    t = pl.program_id(1)
    is_ctx = t == 0
    xin = jnp.where(is_ctx, ctx_ref[0], x_ref[0])
    mod = mod_ref[pl.ds(jnp.where(is_ctx, BATCH, b), 1), :]
    shift, scale = mod[:, :D_MODEL], mod[:, D_MODEL:2 * D_MODEL]
    xm = (xin * (1.0 + scale) + shift).astype(BF16)
    hT = lax.dot_general(winT_ref[...], xm, (((1,), (1,)), ((), ())), preferred_element_type=F32)
    o = 0
    parts = []
    for w in EVEN_SPLITS:
        parts.append(hT[o:o + w])
        o += w
    cq, ckv, kr, ga, qb, kb, vb, gb = parts
    tt = xin.shape[0]
    zpad_a = jnp.zeros((HEAD_PAD - A_NOPE - A_ROPE, tt), F32)
    zpad_b = jnp.zeros((HEAD_PAD - B_HD, tt), F32)
    tab_a = tab_a_ref[...]
    tab_b = tab_b_ref[...]

    cqn = _rms_cols(cq, gq_ref[...]).astype(BF16)
    ckvn = _rms_cols(ckv, gkv_ref[...]).astype(BF16)
    qa = jnp.dot(wuqT_ref[...], cqn, preferred_element_type=F32)
    kn = jnp.dot(wukT_ref[...], ckvn, preferred_element_type=F32)
    va = jnp.dot(wuvT_ref[...], ckvn, preferred_element_type=F32)
    k_pe = _rope_cols(kr, tab_a)
    dq = A_NOPE + A_ROPE
    qa_heads, ka_heads = [], []
    for h in range(A_HEADS):
        qa_heads += [qa[dq * h:dq * h + A_NOPE],
                     _rope_cols(qa[dq * h + A_NOPE:dq * (h + 1)], tab_a), zpad_a]
        ka_heads += [kn[A_NOPE * h:A_NOPE * (h + 1)], k_pe, zpad_a]
    ka_ref[0] = jnp.concatenate(ka_heads, axis=0).T.astype(BF16)
    ones_pad = jnp.where(lax.broadcasted_iota(jnp.int32, (V_ROWS - A_V, tt), 0) == 0, 1.0, 0.0)
    va_heads = []
    for h in range(A_HEADS):
        va_heads += [va[A_V * h:A_V * (h + 1)], ones_pad]
    va_ref[0, 0] = jnp.concatenate(va_heads, axis=0).astype(BF16)

    gbk = gbk_ref[...]
    kb_heads = []
    for j in range(B_KV_HEADS):
        kb_heads += [_rope_cols(_rms_cols(kb[B_HD * j:B_HD * (j + 1)], gbk), tab_b), zpad_b]
    kb_ref[0] = jnp.concatenate(kb_heads, axis=0).T.astype(BF16)
    vb_heads = []
    for j in range(B_KV_HEADS):
        vb_heads += [vb[B_HD * j:B_HD * (j + 1)], ones_pad]
    vb_ref[0, 0] = jnp.concatenate(vb_heads, axis=0).astype(BF16)

    @pl.when(t > 0)
    def _():
        qa_ref[0] = (jnp.concatenate(qa_heads, axis=0) * (A_SCALE * LOG2E)).astype(BF16)
        gbq = gbq_ref[...]
        qb_heads = []
        for h in range(B_HEADS):
            qb_heads += [_rope_cols(_rms_cols(qb[B_HD * h:B_HD * (h + 1)], gbq), tab_b), zpad_b]
        qb_ref[0] = (jnp.concatenate(qb_heads, axis=0) * (B_SCALE * LOG2E)).astype(BF16)
        g_ref[0] = jnp.concatenate([jax.nn.silu(ga), jax.nn.silu(gb)], axis=0).astype(BF16)


def _even_projection(x, ctx, mod, winT, gq, gkv, wuqT, wukT, wuvT, gbq, gbk, tab_a, tab_b):
    tt = TOK_TILE
    nt = N_TOK_TILES
    lat = lambda b, t: (b, 0, jnp.maximum(t - 1, 0))
    full2 = lambda b, t: (0, 0)
    out_shape = (
        jax.ShapeDtypeStruct((BATCH, A_HEADS * HEAD_PAD, SEQ), BF16),
        jax.ShapeDtypeStruct((BATCH, T_ALL, A_HEADS * HEAD_PAD), BF16),
        jax.ShapeDtypeStruct((BATCH, nt, A_HEADS * V_ROWS, tt), BF16),
        jax.ShapeDtypeStruct((BATCH, B_HEADS * HEAD_PAD, SEQ), BF16),
        jax.ShapeDtypeStruct((BATCH, T_ALL, B_KV_HEADS * HEAD_PAD), BF16),
        jax.ShapeDtypeStruct((BATCH, nt, B_KV_HEADS * V_ROWS, tt), BF16),
        jax.ShapeDtypeStruct((BATCH, A_WIDTH + B_WIDTH, SEQ), BF16),
    )
    out_specs = (
        pl.BlockSpec((1, A_HEADS * HEAD_PAD, tt), lat),
        pl.BlockSpec((1, tt, A_HEADS * HEAD_PAD), lambda b, t: (b, t, 0)),
        pl.BlockSpec((1, 1, A_HEADS * V_ROWS, tt), lambda b, t: (b, t, 0, 0)),
        pl.BlockSpec((1, B_HEADS * HEAD_PAD, tt), lat),
        pl.BlockSpec((1, tt, B_KV_HEADS * HEAD_PAD), lambda b, t: (b, t, 0)),
        pl.BlockSpec((1, 1, B_KV_HEADS * V_ROWS, tt), lambda b, t: (b, t, 0, 0)),
        pl.BlockSpec((1, A_WIDTH + B_WIDTH, tt), lat),
    )
    in_specs = [
        pl.BlockSpec((1, tt, D_MODEL), lambda b, t: (b, jnp.maximum(t - 1, 0), 0)),
        pl.BlockSpec((1, CTX_LEN, D_MODEL), lambda b, t: (b, 0, 0)),
        pl.BlockSpec(mod.shape, full2),
        pl.BlockSpec(winT.shape, full2),
        pl.BlockSpec(gq.shape, full2),
        pl.BlockSpec(gkv.shape, full2),
        pl.BlockSpec(wuqT.shape, full2),
        pl.BlockSpec(wukT.shape, full2),
        pl.BlockSpec(wuvT.shape, full2),
        pl.BlockSpec(gbq.shape, full2),
        pl.BlockSpec(gbk.shape, full2),
        pl.BlockSpec((A_ROPE, tt), lambda b, t: (0, t)),
        pl.BlockSpec((B_HD, tt), lambda b, t: (0, t)),
    ]
    return pl.pallas_call(
        _even_proj_kernel,
        grid=(BATCH, nt),
        in_specs=in_specs,
        out_specs=out_specs,
        out_shape=out_shape,
        compiler_params=pltpu.CompilerParams(
            dimension_semantics=("arbitrary", "arbitrary"), vmem_limit_bytes=VMEM_LIMIT),
        name="even_projection",
    )(x, ctx, mod, winT, gq, gkv, wuqT, wukT, wuvT, gbq, gbk, tab_a, tab_b)


def _attn_kernel(q_ref, k_ref, v_ref, g_ref, o_ref, s0_ref, s1_ref, p0_ref, p1_ref, acc_ref):
    q = q_ref[0]
    tq = q.shape[1]
    dv = o_ref.shape[1]
    s_refs = (s0_ref, s1_ref)
    p_refs = (p0_ref, p1_ref)

    def scores(j, s_ref):
        k = k_ref[0, pl.ds(pl.multiple_of(j * TOK_TILE, TOK_TILE), TOK_TILE), :]
        s_ref[...] = jnp.dot(k, q, preferred_element_type=F32)

    def values(j, p_ref, alpha):
        acc_ref[...] = alpha * acc_ref[...] + jnp.dot(v_ref[0, j], p_ref[...],
                                                      preferred_element_type=F32)

    def softmax(s_ref, p_ref, m):
        s = s_ref[...]
        m_new = jnp.maximum(m, jnp.max(s, axis=0, keepdims=True))
        p_ref[...] = jnp.exp2(s - m_new).astype(BF16)
        return m_new, jnp.exp2(m - m_new)

    def step(j, u, m, alpha, issue_scores=True):
        if issue_scores:
            scores(j + 1, s_refs[(u + 1) % 2])
        values(jnp.maximum(j - 1, 0), p_refs[(u + 1) % 2], alpha)
        return softmax(s_refs[u % 2], p_refs[u % 2], m)

    acc_ref[...] = jnp.zeros_like(acc_ref)
    p1_ref[...] = jnp.zeros_like(p1_ref)
    scores(0, s0_ref)

    def trip(i, carry):
        m, alpha = carry
        for u in range(ATTN_UNROLL):
            m, alpha = step(i * ATTN_UNROLL + u, u, m, alpha)
        return m, alpha

    n_main = N_TOK_TILES - 1
    m, alpha = lax.fori_loop(0, n_main // ATTN_UNROLL, trip,
                             (jnp.full((1, tq), NEG_BIG, F32), jnp.ones((1, tq), F32)))
    m, alpha = step(n_main, n_main, m, alpha, issue_scores=False)
    values(n_main, p_refs[n_main % 2], alpha)
    acc = acc_ref[...]
    o_ref[0] = (acc[:dv] / acc[dv:dv + 1] * g_ref[0].astype(F32)).astype(BF16)


def _attention(qT, k, vT, gT, *, group, gate_row0, name):
    n_heads = qT.shape[1] // HEAD_PAD
    dv = A_V
    g_blk0 = gate_row0 // dv
    assert (N_TOK_TILES - 1) % ATTN_UNROLL == 0 and ATTN_UNROLL % 2 == 0
    return pl.pallas_call(
        _attn_kernel,
        grid=(BATCH, n_heads, SEQ // Q_TILE),
        in_specs=[
            pl.BlockSpec((1, HEAD_PAD, Q_TILE), lambda b, h, i: (b, h, i)),
            pl.BlockSpec((1, T_ALL, HEAD_PAD), lambda b, h, i: (b, 0, h // group)),
            pl.BlockSpec((1, N_TOK_TILES, V_ROWS, TOK_TILE), lambda b, h, i: (b, 0, h // group, 0)),
            pl.BlockSpec((1, dv, Q_TILE), lambda b, h, i: (b, h + g_blk0, i)),
        ],
        out_specs=pl.BlockSpec((1, dv, Q_TILE), lambda b, h, i: (b, h, i)),
        out_shape=jax.ShapeDtypeStruct((BATCH, n_heads * dv, SEQ), BF16),
        scratch_shapes=[pltpu.VMEM((TOK_TILE, Q_TILE), F32), pltpu.VMEM((TOK_TILE, Q_TILE), F32),
                        pltpu.VMEM((TOK_TILE, Q_TILE), BF16), pltpu.VMEM((TOK_TILE, Q_TILE), BF16),
                        pltpu.VMEM((V_ROWS, Q_TILE), F32)],
        compiler_params=pltpu.CompilerParams(
            dimension_semantics=("arbitrary", "arbitrary", "arbitrary"),
            vmem_limit_bytes=VMEM_LIMIT),
        name=name,
    )(qT, k, vT, gT)


def _post_kernel(ma_ref, mb_ref, x_ref, mod_e_ref, mod_o_ref, woe_ref, lnge_ref, lnbe_ref,
                 wino_ref, vlng_ref, vlnb_ref, ws_ref, bs_ref, woo_ref, lngo_ref, lnbo_ref,
                 out_ref, z_ref):
    b = pl.program_id(0)
    tn = (((0,), (0,)), ((), ()))
    y = (lax.dot_general(ma_ref[0], woe_ref[0:A_WIDTH, :], tn, preferred_element_type=F32)
         + lax.dot_general(mb_ref[0], woe_ref[A_WIDTH:, :], tn, preferred_element_type=F32))
    gate_e = mod_e_ref[pl.ds(b, 1), :][:, 2 * D_MODEL:]
    x1 = _layer_norm_rows(DEEPNORM_ALPHA * x_ref[0] + gate_e * y, lnge_ref[...], lnbe_ref[...])

    mo = mod_o_ref[pl.ds(b, 1), :]
    shift, scale, gate = mo[:, :D_MODEL], mo[:, D_MODEL:2 * D_MODEL], mo[:, 2 * D_MODEL:]
    xm = (x1 * (1.0 + scale) + shift).astype(BF16)
    h = jnp.dot(xm, wino_ref[...], preferred_element_type=F32)
    u = jax.nn.gelu(h[:, :C_WIDTH])
    v = _layer_norm_rows(jax.nn.gelu(h[:, C_WIDTH:2 * C_WIDTH]), vlng_ref[...], vlnb_ref[...])
    v = v.astype(BF16)
    ug = u * jax.nn.silu(h[:, 2 * C_WIDTH:])
    for n in range(x1.shape[0] // C_CHUNK):
        rows = slice(n * C_CHUNK, (n + 1) * C_CHUNK)
        for gi in range(C_GROUPS):
            cols = slice(gi * C_GROUP_W, (gi + 1) * C_GROUP_W)
            mixed = jnp.dot(ws_ref[gi], v[rows, cols], preferred_element_type=F32) + bs_ref[gi]
            z_ref[rows, cols] = (ug[rows, cols] * mixed).astype(BF16)
    y2 = jnp.dot(z_ref[...], woo_ref[...], preferred_element_type=F32)
    out_ref[0] = _layer_norm_rows(DEEPNORM_ALPHA * x1 + gate * y2, lngo_ref[...], lnbo_ref[...])


def _post(ma, mb, x, mod_e, mod_o, woe, lnge, lnbe, wino, vlng, vlnb, ws, bs, woo, lngo, lnbo):
    tt = POST_TILE
    full2 = lambda b, t: (0, 0)
    full3 = lambda b, t: (0, 0, 0)
    in_specs = [
        pl.BlockSpec((1, A_WIDTH, tt), lambda b, t: (b, 0, t)),
        pl.BlockSpec((1, B_WIDTH, tt), lambda b, t: (b, 0, t)),
        pl.BlockSpec((1, tt, D_MODEL), lambda b, t: (b, t, 0)),
        pl.BlockSpec(mod_e.shape, full2),
        pl.BlockSpec(mod_o.shape, full2),
        pl.BlockSpec(woe.shape, full2),
        pl.BlockSpec(lnge.shape, full2),
        pl.BlockSpec(lnbe.shape, full2),
        pl.BlockSpec(wino.shape, full2),
        pl.BlockSpec(vlng.shape, full2),
        pl.BlockSpec(vlnb.shape, full2),
        pl.BlockSpec(ws.shape, full3),
        pl.BlockSpec(bs.shape, full3),
        pl.BlockSpec(woo.shape, full2),
        pl.BlockSpec(lngo.shape, full2),
        pl.BlockSpec(lnbo.shape, full2),
    ]
    return pl.pallas_call(
        _post_kernel,
        grid=(BATCH, SEQ // tt),
        in_specs=in_specs,
        out_specs=pl.BlockSpec((1, tt, D_MODEL), lambda b, t: (b, t, 0)),
        out_shape=jax.ShapeDtypeStruct((BATCH, SEQ, D_MODEL), F32),
        scratch_shapes=[pltpu.VMEM((tt, C_WIDTH), BF16)],
        compiler_params=pltpu.CompilerParams(
            dimension_semantics=("arbitrary", "arbitrary"), vmem_limit_bytes=VMEM_LIMIT),
        name="out_proj_odd_layer",
    )(ma, mb, x, mod_e, mod_o, woe, lnge, lnbe, wino, vlng, vlnb, ws, bs, woo, lngo, lnbo)


def _rope_table(d_rot):
    rows = SEQ // GRID_W
    row = jnp.repeat(jnp.arange(rows, dtype=F32), GRID_W)
    col = jnp.tile(jnp.arange(GRID_W, dtype=F32), rows)
    d_axis = d_rot // 2
    inv_freq = ROPE_THETA ** (-jnp.arange(0, d_axis, 2, dtype=F32) / d_axis)
    ang_r = row[None, :] * inv_freq[:, None]
    ang_c = col[None, :] * inv_freq[:, None]
    lat = jnp.concatenate([jnp.cos(ang_r), jnp.sin(ang_r), jnp.cos(ang_c), jnp.sin(ang_c)], axis=0)
    one = jnp.ones((d_axis // 2, CTX_LEN), F32)
    zero = jnp.zeros((d_axis // 2, CTX_LEN), F32)
    return jnp.concatenate([jnp.concatenate([one, zero, one, zero], axis=0), lat], axis=1)


def kernel(x, c, ctx, c_ctx, e_w_mod, e_b_mod, e_w_in, e_a_q_norm, e_a_kv_norm, e_a_w_uq, e_a_w_ukv, e_b_q_norm, e_b_k_norm, e_w_out, e_ln_g, e_ln_b, o_w_mod, o_b_mod, o_w_in, o_v_ln_g, o_v_ln_b, o_w_s, o_b_s, o_w_out, o_ln_g, o_ln_b):
    cond = jnp.concatenate(
        [c, c_ctx[None, :], jnp.zeros((MOD_ROWS - BATCH - 1, D_MODEL), F32)], axis=0)
    mod_e = _modulation(cond, e_w_mod[0], e_b_mod[0])
    mod_o = _modulation(cond, o_w_mod[0], o_b_mod[0])

    winT = e_w_in[0].T.astype(BF16)
    wuqT = e_a_w_uq[0].T.astype(BF16)
    wukv = e_a_w_ukv[0].reshape(A_KV_RANK, A_HEADS, A_NOPE + A_V)
    wukT = wukv[:, :, :A_NOPE].reshape(A_KV_RANK, A_HEADS * A_NOPE).T.astype(BF16)
    wuvT = wukv[:, :, A_NOPE:].reshape(A_KV_RANK, A_HEADS * A_V).T.astype(BF16)
    col = lambda v: v.reshape(-1, 1)
    row = lambda v: v.reshape(1, -1)

    qa, ka, va, qb, kb, vb, gates = _even_projection(
        x, ctx, mod_e, winT, col(e_a_q_norm[0]), col(e_a_kv_norm[0]), wuqT, wukT, wuvT,
        col(e_b_q_norm[0]), col(e_b_k_norm[0]), _rope_table(A_ROPE), _rope_table(B_HD))

    ma = _attention(qa, ka, va, gates, group=1, gate_row0=0, name="attention_mla")
    mb = _attention(qb, kb, vb, gates, group=B_GROUP, gate_row0=A_WIDTH, name="attention_gqa")

    bs = jnp.broadcast_to(o_b_s[0][:, :, None], (C_GROUPS, C_CHUNK, C_GROUP_W))
    return _post(ma, mb, x, mod_e, mod_o, e_w_out[0].astype(BF16), row(e_ln_g[0]), row(e_ln_b[0]),
                 o_w_in[0].astype(BF16), row(o_v_ln_g[0]), row(o_v_ln_b[0]),
                 o_w_s[0].astype(BF16), bs, o_w_out[0].astype(BF16), row(o_ln_g[0]), row(o_ln_b[0]))
```

```python
import functools

import jax
import jax.numpy as jnp
from jax import lax
from jax.experimental import pallas as pl
from jax.experimental.pallas import tpu as pltpu

D_MODEL = 1024
BATCH = 4
SEQ = 8192
DEPTH = 2
GRID_W = 64
CTX_LEN = 256
ROPE_THETA = 10000.0
EPS = 1e-6

A_HEADS = 8
A_NOPE = 64
A_ROPE = 32
A_V = 64
A_Q_RANK = 384
A_KV_RANK = 256
A_WIDTH = A_HEADS * A_V
A_SCALE = (A_NOPE + A_ROPE) ** -0.5

B_HEADS = 8
B_KV_HEADS = 2
B_GROUP = B_HEADS // B_KV_HEADS
B_HD = 64
B_WIDTH = B_HEADS * B_HD
B_SCALE = B_HD ** -0.5

EVEN_SPLITS = [A_Q_RANK, A_KV_RANK, A_ROPE, A_WIDTH,
               B_HEADS * B_HD, B_KV_HEADS * B_HD, B_KV_HEADS * B_HD, B_WIDTH]
EVEN_IN = sum(EVEN_SPLITS)

C_CHUNK = 128
C_GROUPS = 8
C_WIDTH = 1024
C_GROUP_W = C_WIDTH // C_GROUPS
ODD_IN = 3 * C_WIDTH

DEEPNORM_ALPHA = (2 * DEPTH) ** 0.25

HEAD_PAD = 128
MOD_ROWS = 8
CTX_MOD_ROW = BATCH

TOK_TILE = 256
PROJ_SUBS = 2
N_LAT_TILES = SEQ // TOK_TILE
N_KEY_TILES = (CTX_LEN + SEQ) // TOK_TILE
assert CTX_LEN == TOK_TILE
Q_TILE = 512
N_Q_TILES = SEQ // Q_TILE
Q_UNROLL = 2
V_ROWS = 80
LOG2E = 1.4426950408889634
POST_TILE = 512
POST_SUB = 256
MOD_COL_TILE = 768
NEG_BIG = -1e30
VMEM_LIMIT = 56 * 1024 * 1024

F32 = jnp.float32
BF16 = jnp.bfloat16


def _layer_norm_rows(xf, g, b):
    mu = jnp.mean(xf, axis=-1, keepdims=True)
    var = jnp.mean(jnp.square(xf - mu), axis=-1, keepdims=True)
    return (xf - mu) * lax.rsqrt(var + EPS) * g + b


def _run_staggered(stage_generators):
    pending = list(stage_generators)
    started = 0
    while pending:
        started = min(started + 1, len(pending))
        for gen in list(pending[:started]):
            try:
                next(gen)
            except StopIteration:
                pending.remove(gen)
                started -= 1


def _mod_kernel(c_ref, w_ref, b_ref, o_ref):
    a = jax.nn.silu(c_ref[...])
    o_ref[...] = jnp.dot(a, w_ref[...], preferred_element_type=F32,
                         precision=lax.Precision.HIGHEST) + b_ref[...]


def _modulation(cond, w_mod, b_mod):
    n = w_mod.shape[1]
    return pl.pallas_call(
        _mod_kernel,
        grid=(n // MOD_COL_TILE,),
        in_specs=[pl.BlockSpec((MOD_ROWS, D_MODEL), lambda j: (0, 0)),
                  pl.BlockSpec((D_MODEL, MOD_COL_TILE), lambda j: (0, j)),
                  pl.BlockSpec((1, MOD_COL_TILE), lambda j: (0, j))],
        out_specs=pl.BlockSpec((MOD_ROWS, MOD_COL_TILE), lambda j: (0, j)),
        out_shape=jax.ShapeDtypeStruct((MOD_ROWS, n), F32),
        name="modulation",
    )(cond, w_mod, b_mod.reshape(1, n))


def _rms_cols(v, g):
    ms = jnp.mean(v * v, axis=0, keepdims=True)
    return v * lax.rsqrt(ms + EPS) * g


def _rope_cols(p, tab):
    if tab is None:
        return p
    q = p.shape[0] // 4
    x1r, x2r, x1c, x2c = p[0:q], p[q:2 * q], p[2 * q:3 * q], p[3 * q:4 * q]
    cr, sr, cc, sc = tab[0:q], tab[q:2 * q], tab[2 * q:3 * q], tab[3 * q:4 * q]
    return jnp.concatenate([x1r * cr - x2r * sr, x2r * cr + x1r * sr,
                            x1c * cc - x2c * sc, x2c * cc + x1c * sc], axis=0)


def _even_proj_kernel(*refs, latent):
    if latent:
        (x_ref, mod_ref, winT_ref, gq_ref, gkv_ref, wuqT_ref, wukT_ref, wuvT_ref, gbq_ref, gbk_ref,
         tab_a_ref, tab_b_ref, ka_ref, va_ref, kb_ref, vb_ref, qa_ref, qb_ref, g_ref) = refs
        mod_row = pl.program_id(0)
    else:
        (x_ref, mod_ref, winT_ref, gkv_ref, wukT_ref, wuvT_ref, gbk_ref,
         ka_ref, va_ref, kb_ref, vb_ref) = refs
        mod_row = CTX_MOD_ROW
    mod = mod_ref[pl.ds(mod_row, 1), :]
    shift, scale = mod[:, :D_MODEL], mod[:, D_MODEL:2 * D_MODEL]
    tt = TOK_TILE
    zpad_a = jnp.zeros((HEAD_PAD - A_NOPE - A_ROPE, tt), F32)
    zpad_b = jnp.zeros((HEAD_PAD - B_HD, tt), F32)
    ones_pad = jnp.where(lax.broadcasted_iota(jnp.int32, (V_ROWS - A_V, tt), 0) == 0, 1.0, 0.0)
    dq = A_NOPE + A_ROPE

    def sub_tile(sub):
        tok = slice(sub * tt, (sub + 1) * tt)
        xm = (x_ref[0, tok, :] * (1.0 + scale) + shift).astype(BF16)
        yield
        hT = lax.dot_general(winT_ref[...], xm, (((1,), (1,)), ((), ())),
                             preferred_element_type=F32)
        yield
        o = 0
        parts = []
        for w in EVEN_SPLITS:
            parts.append(hT[o:o + w])
            o += w
        cq, ckv, kr, ga, qb, kb, vb, gb = parts
        tab_a = tab_a_ref[:, tok] if latent else None
        tab_b = tab_b_ref[:, tok] if latent else None
        ckvn = _rms_cols(ckv, gkv_ref[...]).astype(BF16)
        if latent:
            cqn = _rms_cols(cq, gq_ref[...]).astype(BF16)
        gbk = gbk_ref[...]
        kb_heads, vb_heads = [], []
        for j in range(B_KV_HEADS):
            kb_heads += [_rope_cols(_rms_cols(kb[B_HD * j:B_HD * (j + 1)], gbk), tab_b), zpad_b]
            vb_heads += [vb[B_HD * j:B_HD * (j + 1)], ones_pad]
        kb_ref[0, tok, :] = jnp.concatenate(kb_heads, axis=0).T.astype(BF16)
        vb_ref[0, sub] = jnp.concatenate(vb_heads, axis=0).astype(BF16)
        if latent:
            gbq = gbq_ref[...]
            for h in range(B_HEADS):
                qh = jnp.concatenate(
                    [_rope_cols(_rms_cols(qb[B_HD * h:B_HD * (h + 1)], gbq), tab_b), zpad_b], axis=0)
                qb_ref[0, h, 0, :, tok] = (qh * (B_SCALE * LOG2E)).astype(BF16)
                g_ref[0, A_HEADS + h, 0, :, tok] = jax.nn.silu(
                    gb[B_HD * h:B_HD * (h + 1)]).astype(BF16)
            for h in range(A_HEADS):
                g_ref[0, h, 0, :, tok] = jax.nn.silu(ga[A_V * h:A_V * (h + 1)]).astype(BF16)
        yield
        kn = jnp.dot(wukT_ref[...], ckvn, preferred_element_type=F32)
        va = jnp.dot(wuvT_ref[...], ckvn, preferred_element_type=F32)
        if latent:
            qa = jnp.dot(wuqT_ref[...], cqn, preferred_element_type=F32)
        yield
        k_pe = _rope_cols(kr, tab_a)
        ka_heads, va_heads = [], []
        for h in range(A_HEADS):
            ka_heads += [kn[A_NOPE * h:A_NOPE * (h + 1)], k_pe, zpad_a]
            va_heads += [va[A_V * h:A_V * (h + 1)], ones_pad]
        ka_ref[0, tok, :] = jnp.concatenate(ka_heads, axis=0).T.astype(BF16)
        va_ref[0, sub] = jnp.concatenate(va_heads, axis=0).astype(BF16)
        if latent:
            for h in range(A_HEADS):
                qh = jnp.concatenate(
                    [qa[dq * h:dq * h + A_NOPE],
                     _rope_cols(qa[dq * h + A_NOPE:dq * (h + 1)], tab_a), zpad_a], axis=0)
                qa_ref[0, h, 0, :, tok] = (qh * (A_SCALE * LOG2E)).astype(BF16)

    _run_staggered(sub_tile(s) for s in range(x_ref.shape[1] // tt))


def _even_projection(x, mod, winT, gq, gkv, wuqT, wukT, wuvT, gbq, gbk, tab_a, tab_b, *, latent):
    n_tok = x.shape[1]
    blk = PROJ_SUBS * TOK_TILE if latent else CTX_LEN
    subs = blk // TOK_TILE
    assert Q_TILE == blk or not latent
    full2 = lambda b, t: (0, 0)
    kv_shapes = (
        jax.ShapeDtypeStruct((BATCH, n_tok, A_HEADS * HEAD_PAD), BF16),
        jax.ShapeDtypeStruct((BATCH, n_tok // TOK_TILE, A_HEADS * V_ROWS, TOK_TILE), BF16),
        jax.ShapeDtypeStruct((BATCH, n_tok, B_KV_HEADS * HEAD_PAD), BF16),
        jax.ShapeDtypeStruct((BATCH, n_tok // TOK_TILE, B_KV_HEADS * V_ROWS, TOK_TILE), BF16),
    )
    kv_specs = (
        pl.BlockSpec((1, blk, A_HEADS * HEAD_PAD), lambda b, t: (b, t, 0)),
        pl.BlockSpec((1, subs, A_HEADS * V_ROWS, TOK_TILE), lambda b, t: (b, t, 0, 0)),
        pl.BlockSpec((1, blk, B_KV_HEADS * HEAD_PAD), lambda b, t: (b, t, 0)),
        pl.BlockSpec((1, subs, B_KV_HEADS * V_ROWS, TOK_TILE), lambda b, t: (b, t, 0, 0)),
    )
    x_spec = pl.BlockSpec((1, blk, D_MODEL), lambda b, t: (b, t, 0))
    whole = lambda a: pl.BlockSpec(a.shape, full2)
    if latent:
        q_tile = lambda b, t: (b, 0, t, 0, 0)
        operands = (x, mod, winT, gq, gkv, wuqT, wukT, wuvT, gbq, gbk, tab_a, tab_b)
        in_specs = [x_spec] + [whole(a) for a in operands[1:10]] + [
            pl.BlockSpec((A_ROPE, blk), lambda b, t: (0, t)),
            pl.BlockSpec((B_HD, blk), lambda b, t: (0, t))]
        out_shape = kv_shapes + (
            jax.ShapeDtypeStruct((BATCH, A_HEADS, N_Q_TILES, HEAD_PAD, Q_TILE), BF16),
            jax.ShapeDtypeStruct((BATCH, B_HEADS, N_Q_TILES, HEAD_PAD, Q_TILE), BF16),
            jax.ShapeDtypeStruct((BATCH, A_HEADS + B_HEADS, N_Q_TILES, A_V, Q_TILE), BF16),
        )
        out_specs = kv_specs + (
            pl.BlockSpec((1, A_HEADS, 1, HEAD_PAD, Q_TILE), q_tile),
            pl.BlockSpec((1, B_HEADS, 1, HEAD_PAD, Q_TILE), q_tile),
            pl.BlockSpec((1, A_HEADS + B_HEADS, 1, A_V, Q_TILE), q_tile),
        )
    else:
        operands = (x, mod, winT, gkv, wukT, wuvT, gbk)
        in_specs = [x_spec] + [whole(a) for a in operands[1:]]
        out_shape, out_specs = kv_shapes, kv_specs
    return pl.pallas_call(
        functools.partial(_even_proj_kernel, latent=latent),
        grid=(BATCH, n_tok // blk),
        in_specs=in_specs,
        out_specs=out_specs,
        out_shape=out_shape,
        compiler_params=pltpu.CompilerParams(
            dimension_semantics=("arbitrary", "arbitrary"), vmem_limit_bytes=VMEM_LIMIT),
        name="even_projection_latent" if latent else "even_projection_context",
    )(*operands)


def _attn_kernel(q_ref, kc_ref, kl_ref, vc_ref, vl_ref, g_ref, o_ref,
                 s_first_ref, s_ring_ref, p_ring_ref, acc_ref):
    n_q = q_ref.shape[2]
    tq = q_ref.shape[4]
    dv = o_ref.shape[3]
    n_k = N_KEY_TILES

    def s_buf(j):
        return s_first_ref if j == 0 else s_ring_ref.at[j % 2]

    def p_buf(j):
        return p_ring_ref.at[j % 2]

    def scores(qi, j, s_ref):
        k = kc_ref[0] if j == 0 else kl_ref[0, (j - 1) * TOK_TILE:j * TOK_TILE, :]
        s_ref[...] = jnp.dot(k, q_ref[0, 0, qi], preferred_element_type=F32)

    def values(j, alpha):
        v = vc_ref[0, 0] if j == 0 else vl_ref[0, j - 1]
        pv = jnp.dot(v, p_buf(j)[...], preferred_element_type=F32)
        acc_ref[...] = pv if alpha is None else alpha * acc_ref[...] + pv

    def softmax(j, m):
        s = s_buf(j)[...]
        m_new = jnp.maximum(m, jnp.max(s, axis=0, keepdims=True))
        p_buf(j)[...] = jnp.exp2(s - m_new).astype(BF16)
        return m_new, jnp.exp2(m - m_new)

    def query_tile(qi):
        m = jnp.full((1, tq), NEG_BIG, F32)
        alpha = None
        for j in range(n_k):
            if j + 1 < n_k:
                scores(qi, j + 1, s_buf(j + 1))
            else:
                scores(jnp.minimum(qi + 1, n_q - 1), 0, s_first_ref)
            if j >= 1:
                values(j - 1, alpha if j >= 2 else None)
            m, alpha = softmax(j, m)
        values(n_k - 1, alpha)
        acc = acc_ref[...]
        o_ref[0, 0, qi] = (acc[:dv] / acc[dv:dv + 1] * g_ref[0, 0, qi].astype(F32)).astype(BF16)

    def query_tiles(i, carry):
        for u in range(Q_UNROLL):
            query_tile(i * Q_UNROLL + u)
        return carry

    scores(0, 0, s_first_ref)
    lax.fori_loop(0, n_q // Q_UNROLL, query_tiles, 0)


def _attention(qT, kc, kl, vcT, vlT, gT, *, group, gate_head0, name):
    n_heads = qT.shape[1]
    dv = A_V
    kv_head = lambda b, h: (b, 0, h // group)
    v_head = lambda b, h: (b, 0, h // group, 0)
    return pl.pallas_call(
        _attn_kernel,
        grid=(BATCH, n_heads),
        in_specs=[
            pl.BlockSpec((1, 1, N_Q_TILES, HEAD_PAD, Q_TILE), lambda b, h: (b, h, 0, 0, 0)),
            pl.BlockSpec((1, CTX_LEN, HEAD_PAD), kv_head),
            pl.BlockSpec((1, SEQ, HEAD_PAD), kv_head),
            pl.BlockSpec((1, 1, V_ROWS, TOK_TILE), v_head),
            pl.BlockSpec((1, N_LAT_TILES, V_ROWS, TOK_TILE), v_head),
            pl.BlockSpec((1, 1, N_Q_TILES, dv, Q_TILE), lambda b, h: (b, h + gate_head0, 0, 0, 0)),
        ],
        out_specs=pl.BlockSpec((1, 1, N_Q_TILES, dv, Q_TILE), lambda b, h: (b, h, 0, 0, 0)),
        out_shape=jax.ShapeDtypeStruct((BATCH, n_heads, N_Q_TILES, dv, Q_TILE), BF16),
        scratch_shapes=[pltpu.VMEM((TOK_TILE, Q_TILE), F32),
                        pltpu.VMEM((2, TOK_TILE, Q_TILE), F32),
                        pltpu.VMEM((2, TOK_TILE, Q_TILE), BF16),
                        pltpu.VMEM((V_ROWS, Q_TILE), F32)],
        compiler_params=pltpu.CompilerParams(
            dimension_semantics=("arbitrary", "arbitrary"), vmem_limit_bytes=VMEM_LIMIT),
        name=name,
    )(qT, kc, kl, vcT, vlT, gT)


def _post_kernel(ma_ref, mb_ref, x_ref, mod_e_ref, mod_o_ref, woe_ref, lnge_ref, lnbe_ref,
                 wino_ref, vlng_ref, vlnb_ref, ws_ref, bs_ref, woo_ref, lngo_ref, lnbo_ref,
                 out_ref, z_ref):
    b = pl.program_id(0)
    tn = (((0,), (0,)), ((), ()))
    gate_e = mod_e_ref[pl.ds(b, 1), :][:, 2 * D_MODEL:]
    mo = mod_o_ref[pl.ds(b, 1), :]
    shift, scale, gate = mo[:, :D_MODEL], mo[:, D_MODEL:2 * D_MODEL], mo[:, 2 * D_MODEL:]

    def sub_tile(sub):
        tok = slice(sub * POST_SUB, (sub + 1) * POST_SUB)
        ma = ma_ref[0, :, 0, :, tok].reshape(A_WIDTH, POST_SUB)
        mb = mb_ref[0, :, 0, :, tok].reshape(B_WIDTH, POST_SUB)
        y = (lax.dot_general(ma, woe_ref[0:A_WIDTH, :], tn, preferred_element_type=F32)
             + lax.dot_general(mb, woe_ref[A_WIDTH:, :], tn, preferred_element_type=F32))
        yield
        x1 = _layer_norm_rows(DEEPNORM_ALPHA * x_ref[0, tok, :] + gate_e * y,
                              lnge_ref[...], lnbe_ref[...])
        xm = (x1 * (1.0 + scale) + shift).astype(BF16)
        yield
        h = jnp.dot(xm, wino_ref[...], preferred_element_type=F32)
        yield
        u = jax.nn.gelu(h[:, :C_WIDTH])
        v = _layer_norm_rows(jax.nn.gelu(h[:, C_WIDTH:2 * C_WIDTH]), vlng_ref[...], vlnb_ref[...])
        v = v.astype(BF16)
        ug = u * jax.nn.silu(h[:, 2 * C_WIDTH:])
        yield
        for n in range(POST_SUB // C_CHUNK):
            rows = slice(n * C_CHUNK, (n + 1) * C_CHUNK)
            zrows = slice(sub * POST_SUB + n * C_CHUNK, sub * POST_SUB + (n + 1) * C_CHUNK)
            for gi in range(C_GROUPS):
                cols = slice(gi * C_GROUP_W, (gi + 1) * C_GROUP_W)
                mixed = jnp.dot(ws_ref[gi], v[rows, cols], preferred_element_type=F32) + bs_ref[gi]
                z_ref[zrows, cols] = (ug[rows, cols] * mixed).astype(BF16)
        y2 = jnp.dot(z_ref[tok, :], woo_ref[...], preferred_element_type=F32)
        yield
        out_ref[0, tok, :] = _layer_norm_rows(DEEPNORM_ALPHA * x1 + gate * y2,
                                              lngo_ref[...], lnbo_ref[...])

    _run_staggered(sub_tile(s) for s in range(x_ref.shape[1] // POST_SUB))


def _post(ma, mb, x, mod_e, mod_o, woe, lnge, lnbe, wino, vlng, vlnb, ws, bs, woo, lngo, lnbo):
    tt = POST_TILE
    assert tt == Q_TILE
    full2 = lambda b, t: (0, 0)
    full3 = lambda b, t: (0, 0, 0)
    in_specs = [
        pl.BlockSpec((1, A_HEADS, 1, A_V, tt), lambda b, t: (b, 0, t, 0, 0)),
        pl.BlockSpec((1, B_HEADS, 1, B_HD, tt), lambda b, t: (b, 0, t, 0, 0)),
        pl.BlockSpec((1, tt, D_MODEL), lambda b, t: (b, t, 0)),
        pl.BlockSpec(mod_e.shape, full2),
        pl.BlockSpec(mod_o.shape, full2),
        pl.BlockSpec(woe.shape, full2),
        pl.BlockSpec(lnge.shape, full2),
        pl.BlockSpec(lnbe.shape, full2),
        pl.BlockSpec(wino.shape, full2),
        pl.BlockSpec(vlng.shape, full2),
        pl.BlockSpec(vlnb.shape, full2),
        pl.BlockSpec(ws.shape, full3),
        pl.BlockSpec(bs.shape, full3),
        pl.BlockSpec(woo.shape, full2),
        pl.BlockSpec(lngo.shape, full2),
        pl.BlockSpec(lnbo.shape, full2),
    ]
    return pl.pallas_call(
        _post_kernel,
        grid=(BATCH, SEQ // tt),
        in_specs=in_specs,
        out_specs=pl.BlockSpec((1, tt, D_MODEL), lambda b, t: (b, t, 0)),
        out_shape=jax.ShapeDtypeStruct((BATCH, SEQ, D_MODEL), F32),
        scratch_shapes=[pltpu.VMEM((tt, C_WIDTH), BF16)],
        compiler_params=pltpu.CompilerParams(
            dimension_semantics=("arbitrary", "arbitrary"), vmem_limit_bytes=VMEM_LIMIT),
        name="out_proj_odd_layer",
    )(ma, mb, x, mod_e, mod_o, woe, lnge, lnbe, wino, vlng, vlnb, ws, bs, woo, lngo, lnbo)


def _rope_table(d_rot):
    rows = SEQ // GRID_W
    row = jnp.repeat(jnp.arange(rows, dtype=F32), GRID_W)
    col = jnp.tile(jnp.arange(GRID_W, dtype=F32), rows)
    d_axis = d_rot // 2
    inv_freq = ROPE_THETA ** (-jnp.arange(0, d_axis, 2, dtype=F32) / d_axis)
    ang_r = row[None, :] * inv_freq[:, None]
    ang_c = col[None, :] * inv_freq[:, None]
    return jnp.concatenate([jnp.cos(ang_r), jnp.sin(ang_r), jnp.cos(ang_c), jnp.sin(ang_c)], axis=0)


def kernel(x, c, ctx, c_ctx, e_w_mod, e_b_mod, e_w_in, e_a_q_norm, e_a_kv_norm, e_a_w_uq, e_a_w_ukv, e_b_q_norm, e_b_k_norm, e_w_out, e_ln_g, e_ln_b, o_w_mod, o_b_mod, o_w_in, o_v_ln_g, o_v_ln_b, o_w_s, o_b_s, o_w_out, o_ln_g, o_ln_b):
    cond = jnp.concatenate(
        [c, c_ctx[None, :], jnp.zeros((MOD_ROWS - BATCH - 1, D_MODEL), F32)], axis=0)
    mod_e = _modulation(cond, e_w_mod[0], e_b_mod[0])
    mod_o = _modulation(cond, o_w_mod[0], o_b_mod[0])

    winT = e_w_in[0].T.astype(BF16)
    wuqT = e_a_w_uq[0].T.astype(BF16)
    wukv = e_a_w_ukv[0].reshape(A_KV_RANK, A_HEADS, A_NOPE + A_V)
    wukT = wukv[:, :, :A_NOPE].reshape(A_KV_RANK, A_HEADS * A_NOPE).T.astype(BF16)
    wuvT = wukv[:, :, A_NOPE:].reshape(A_KV_RANK, A_HEADS * A_V).T.astype(BF16)
    col = lambda v: v.reshape(-1, 1)
    row = lambda v: v.reshape(1, -1)
    proj_params = (mod_e, winT, col(e_a_q_norm[0]), col(e_a_kv_norm[0]), wuqT, wukT, wuvT,
                   col(e_b_q_norm[0]), col(e_b_k_norm[0]), _rope_table(A_ROPE), _rope_table(B_HD))

    kac, vac, kbc, vbc = _even_projection(ctx, *proj_params, latent=False)
    kal, val, kbl, vbl, qa, qb, gates = _even_projection(x, *proj_params, latent=True)

    ma = _attention(qa, kac, kal, vac, val, gates, group=1, gate_head0=0, name="attention_mla")
    mb = _attention(qb, kbc, kbl, vbc, vbl, gates, group=B_GROUP, gate_head0=A_HEADS,
                    name="attention_gqa")

    bs = jnp.broadcast_to(o_b_s[0][:, :, None], (C_GROUPS, C_CHUNK, C_GROUP_W))
    return _post(ma, mb, x, mod_e, mod_o, e_w_out[0].astype(BF16), row(e_ln_g[0]), row(e_ln_b[0]),
                 o_w_in[0].astype(BF16), row(o_v_ln_g[0]), row(o_v_ln_b[0]),
                 o_w_s[0].astype(BF16), bs, o_w_out[0].astype(BF16), row(o_ln_g[0]), row(o_ln_b[0]))
```

```python
import functools

import jax
import jax.numpy as jnp
from jax import lax
from jax.experimental import pallas as pl
from jax.experimental.pallas import tpu as pltpu

D_MODEL = 1024
BATCH = 4
SEQ = 8192
DEPTH = 2
GRID_W = 64
CTX_LEN = 256
ROPE_THETA = 10000.0
EPS = 1e-6

A_HEADS = 8
A_NOPE = 64
A_ROPE = 32
A_V = 64
A_Q_RANK = 384
A_KV_RANK = 256
A_WIDTH = A_HEADS * A_V
A_SCALE = (A_NOPE + A_ROPE) ** -0.5

B_HEADS = 8
B_KV_HEADS = 2
B_GROUP = B_HEADS // B_KV_HEADS
B_HD = 64
B_WIDTH = B_HEADS * B_HD
B_SCALE = B_HD ** -0.5

EVEN_SPLITS = [A_Q_RANK, A_KV_RANK, A_ROPE, A_WIDTH,
               B_HEADS * B_HD, B_KV_HEADS * B_HD, B_KV_HEADS * B_HD, B_WIDTH]
EVEN_IN = sum(EVEN_SPLITS)
ROW_CKV, ROW_KR, ROW_GA, ROW_QB, ROW_KB, ROW_VB, ROW_GB = (
    sum(EVEN_SPLITS[:i]) for i in range(1, len(EVEN_SPLITS)))

C_CHUNK = 128
C_GROUPS = 8
C_WIDTH = 1024
C_GROUP_W = C_WIDTH // C_GROUPS
ODD_IN = 3 * C_WIDTH

DEEPNORM_ALPHA = (2 * DEPTH) ** 0.25

HEAD_PAD = 128
MOD_ROWS = 8
CTX_MOD_ROW = BATCH

TOK_TILE = 256
PROJ_SUBS = 2
N_LAT_TILES = SEQ // TOK_TILE
N_KEY_TILES = (CTX_LEN + SEQ) // TOK_TILE
assert CTX_LEN == TOK_TILE
Q_TILE = 512
N_Q_TILES = SEQ // Q_TILE
Q_UNROLL = 2
V_ROWS = 80
LOG2E = 1.4426950408889634
POST_TILE = 512
POST_SUB = 256
MOD_COL_TILE = 768
VMEM_LIMIT = 56 * 1024 * 1024

F32 = jnp.float32
BF16 = jnp.bfloat16


def _layer_norm_rows(xf, g, b):
    mu = jnp.mean(xf, axis=-1, keepdims=True)
    var = jnp.mean(jnp.square(xf - mu), axis=-1, keepdims=True)
    return (xf - mu) * lax.rsqrt(var + EPS) * g + b


def _run_staggered(stage_generators):
    pending = list(stage_generators)
    started = 0
    while pending:
        started = min(started + 1, len(pending))
        for gen in list(pending[:started]):
            try:
                next(gen)
            except StopIteration:
                pending.remove(gen)
                started -= 1


def _mod_kernel(c_ref, w_ref, b_ref, o_ref):
    a = jax.nn.silu(c_ref[...])
    o_ref[...] = jnp.dot(a, w_ref[...], preferred_element_type=F32,
                         precision=lax.Precision.HIGHEST) + b_ref[...]


def _modulation(cond, w_mod, b_mod):
    n = w_mod.shape[1]
    return pl.pallas_call(
        _mod_kernel,
        grid=(n // MOD_COL_TILE,),
        in_specs=[pl.BlockSpec((MOD_ROWS, D_MODEL), lambda j: (0, 0)),
                  pl.BlockSpec((D_MODEL, MOD_COL_TILE), lambda j: (0, j)),
                  pl.BlockSpec((1, MOD_COL_TILE), lambda j: (0, j))],
        out_specs=pl.BlockSpec((MOD_ROWS, MOD_COL_TILE), lambda j: (0, j)),
        out_shape=jax.ShapeDtypeStruct((MOD_ROWS, n), F32),
        name="modulation",
    )(cond, w_mod, b_mod.reshape(1, n))


def _rms_cols(v, g):
    ms = jnp.mean(v * v, axis=0, keepdims=True)
    return v * lax.rsqrt(ms + EPS) * g


def _rope_cols(p, tab):
    if tab is None:
        return p
    q = p.shape[0] // 4
    x1r, x2r, x1c, x2c = p[0:q], p[q:2 * q], p[2 * q:3 * q], p[3 * q:4 * q]
    cr, sr, cc, sc = tab[0:q], tab[q:2 * q], tab[2 * q:3 * q], tab[3 * q:4 * q]
    return jnp.concatenate([x1r * cr - x2r * sr, x2r * cr + x1r * sr,
                            x1c * cc - x2c * sc, x2c * cc + x1c * sc], axis=0)


def _even_proj_kernel(*refs, latent):
    if latent:
        (x_ref, mod_ref, winT_ref, gq_ref, gkv_ref, wuqT_ref, wukT_ref, wuvT_ref, gbq_ref, gbk_ref,
         tab_a_ref, tab_b_ref, ka_ref, va_ref, kb_ref, vb_ref, qa_ref, qb_ref, g_ref) = refs
        mod_row = pl.program_id(0)
    else:
        (x_ref, mod_ref, winT_ref, gkv_ref, wukT_ref, wuvT_ref, gbk_ref,
         ka_ref, va_ref, kb_ref, vb_ref) = refs
        mod_row = CTX_MOD_ROW
    mod = mod_ref[pl.ds(mod_row, 1), :]
    shift, scale = mod[:, :D_MODEL], mod[:, D_MODEL:2 * D_MODEL]
    tt = TOK_TILE
    zpad_a = jnp.zeros((HEAD_PAD - A_NOPE - A_ROPE, tt), F32)
    zpad_b = jnp.zeros((HEAD_PAD - B_HD, tt), F32)
    ones_pad = jnp.where(lax.broadcasted_iota(jnp.int32, (V_ROWS - A_V, tt), 0) == 0, 1.0, 0.0)
    dq = A_NOPE + A_ROPE

    def sub_tile(sub):
        tok = slice(sub * tt, (sub + 1) * tt)
        xm = (x_ref[0, tok, :] * (1.0 + scale) + shift).astype(BF16)
        yield

        def project(lo, hi):
            return lax.dot_general(winT_ref[lo:hi, :], xm, (((1,), (1,)), ((), ())),
                                   preferred_element_type=F32)

        cq_ckv_kr = project(0, ROW_GA)
        cq, ckv, kr = cq_ckv_kr[:ROW_CKV], cq_ckv_kr[ROW_CKV:ROW_KR], cq_ckv_kr[ROW_KR:]
        yield
        qb_kb_vb = project(ROW_QB if latent else ROW_KB, ROW_GB)
        kb, vb = qb_kb_vb[-2 * B_KV_HEADS * B_HD:-B_KV_HEADS * B_HD], qb_kb_vb[-B_KV_HEADS * B_HD:]
        qb = qb_kb_vb[:B_WIDTH] if latent else None
        tab_a = tab_a_ref[:, tok] if latent else None
        tab_b = tab_b_ref[:, tok] if latent else None
        ckvn = _rms_cols(ckv, gkv_ref[...]).astype(BF16)
        if latent:
            cqn = _rms_cols(cq, gq_ref[...]).astype(BF16)
        yield
        kn = jnp.dot(wukT_ref[...], ckvn, preferred_element_type=F32)
        va = jnp.dot(wuvT_ref[...], ckvn, preferred_element_type=F32)
        if latent:
            qa = jnp.dot(wuqT_ref[...], cqn, preferred_element_type=F32)
        gbk = gbk_ref[...]
        kb_heads, vb_heads = [], []
        for j in range(B_KV_HEADS):
            kb_heads += [_rope_cols(_rms_cols(kb[B_HD * j:B_HD * (j + 1)], gbk), tab_b), zpad_b]
            vb_heads += [vb[B_HD * j:B_HD * (j + 1)], ones_pad]
        kb_ref[0, tok, :] = jnp.concatenate(kb_heads, axis=0).T.astype(BF16)
        vb_ref[0, sub] = jnp.concatenate(vb_heads, axis=0).astype(BF16)
        if latent:
            gbq = gbq_ref[...]
            for h in range(B_HEADS):
                qh = jnp.concatenate(
                    [_rope_cols(_rms_cols(qb[B_HD * h:B_HD * (h + 1)], gbq), tab_b), zpad_b], axis=0)
                qb_ref[0, h, 0, :, tok] = (qh * (B_SCALE * LOG2E)).astype(BF16)
        yield
        if latent:
            ga = project(ROW_GA, ROW_QB)
            gb = project(ROW_GB, EVEN_IN)
        k_pe = _rope_cols(kr, tab_a)
        ka_heads, va_heads = [], []
        for h in range(A_HEADS):
            ka_heads += [kn[A_NOPE * h:A_NOPE * (h + 1)], k_pe, zpad_a]
            va_heads += [va[A_V * h:A_V * (h + 1)], ones_pad]
        ka_ref[0, tok, :] = jnp.concatenate(ka_heads, axis=0).T.astype(BF16)
        va_ref[0, sub] = jnp.concatenate(va_heads, axis=0).astype(BF16)
        if latent:
            for h in range(A_HEADS):
                qh = jnp.concatenate(
                    [qa[dq * h:dq * h + A_NOPE],
                     _rope_cols(qa[dq * h + A_NOPE:dq * (h + 1)], tab_a), zpad_a], axis=0)
                qa_ref[0, h, 0, :, tok] = (qh * (A_SCALE * LOG2E)).astype(BF16)
            yield
            for h in range(A_HEADS):
                g_ref[0, h, 0, :, tok] = jax.nn.silu(ga[A_V * h:A_V * (h + 1)]).astype(BF16)
            for h in range(B_HEADS):
                g_ref[0, A_HEADS + h, 0, :, tok] = jax.nn.silu(
                    gb[B_HD * h:B_HD * (h + 1)]).astype(BF16)

    _run_staggered(sub_tile(s) for s in range(x_ref.shape[1] // tt))


def _even_projection(x, mod, winT, gq, gkv, wuqT, wukT, wuvT, gbq, gbk, tab_a, tab_b, *, latent):
    n_tok = x.shape[1]
    blk = PROJ_SUBS * TOK_TILE if latent else CTX_LEN
    subs = blk // TOK_TILE
    assert Q_TILE == blk or not latent
    full2 = lambda b, t: (0, 0)
    kv_shapes = (
        jax.ShapeDtypeStruct((BATCH, n_tok, A_HEADS * HEAD_PAD), BF16),
        jax.ShapeDtypeStruct((BATCH, n_tok // TOK_TILE, A_HEADS * V_ROWS, TOK_TILE), BF16),
        jax.ShapeDtypeStruct((BATCH, n_tok, B_KV_HEADS * HEAD_PAD), BF16),
        jax.ShapeDtypeStruct((BATCH, n_tok // TOK_TILE, B_KV_HEADS * V_ROWS, TOK_TILE), BF16),
    )
    kv_specs = (
        pl.BlockSpec((1, blk, A_HEADS * HEAD_PAD), lambda b, t: (b, t, 0)),
        pl.BlockSpec((1, subs, A_HEADS * V_ROWS, TOK_TILE), lambda b, t: (b, t, 0, 0)),
        pl.BlockSpec((1, blk, B_KV_HEADS * HEAD_PAD), lambda b, t: (b, t, 0)),
        pl.BlockSpec((1, subs, B_KV_HEADS * V_ROWS, TOK_TILE), lambda b, t: (b, t, 0, 0)),
    )
    x_spec = pl.BlockSpec((1, blk, D_MODEL), lambda b, t: (b, t, 0))
    whole = lambda a: pl.BlockSpec(a.shape, full2)
    if latent:
        q_tile = lambda b, t: (b, 0, t, 0, 0)
        operands = (x, mod, winT, gq, gkv, wuqT, wukT, wuvT, gbq, gbk, tab_a, tab_b)
        in_specs = [x_spec] + [whole(a) for a in operands[1:10]] + [
            pl.BlockSpec((A_ROPE, blk), lambda b, t: (0, t)),
            pl.BlockSpec((B_HD, blk), lambda b, t: (0, t))]
        out_shape = kv_shapes + (
            jax.ShapeDtypeStruct((BATCH, A_HEADS, N_Q_TILES, HEAD_PAD, Q_TILE), BF16),
            jax.ShapeDtypeStruct((BATCH, B_HEADS, N_Q_TILES, HEAD_PAD, Q_TILE), BF16),
            jax.ShapeDtypeStruct((BATCH, A_HEADS + B_HEADS, N_Q_TILES, A_V, Q_TILE), BF16),
        )
        out_specs = kv_specs + (
            pl.BlockSpec((1, A_HEADS, 1, HEAD_PAD, Q_TILE), q_tile),
            pl.BlockSpec((1, B_HEADS, 1, HEAD_PAD, Q_TILE), q_tile),
            pl.BlockSpec((1, A_HEADS + B_HEADS, 1, A_V, Q_TILE), q_tile),
        )
    else:
        operands = (x, mod, winT, gkv, wukT, wuvT, gbk)
        in_specs = [x_spec] + [whole(a) for a in operands[1:]]
        out_shape, out_specs = kv_shapes, kv_specs
    return pl.pallas_call(
        functools.partial(_even_proj_kernel, latent=latent),
        grid=(BATCH, n_tok // blk),
        in_specs=in_specs,
        out_specs=out_specs,
        out_shape=out_shape,
        compiler_params=pltpu.CompilerParams(
            dimension_semantics=("arbitrary", "arbitrary"), vmem_limit_bytes=VMEM_LIMIT),
        name="even_projection_latent" if latent else "even_projection_context",
    )(*operands)


def _attn_kernel(q_ref, kc_ref, kl_ref, vc_ref, vl_ref, g_ref, o_ref, s_first_ref):
    n_q = q_ref.shape[2]
    tq = q_ref.shape[4]
    dv = o_ref.shape[3]
    n_k = N_KEY_TILES

    def scores(qi, j):
        k = kc_ref[0] if j == 0 else kl_ref[0, (j - 1) * TOK_TILE:j * TOK_TILE, :]
        return jnp.dot(k, q_ref[0, 0, qi], preferred_element_type=F32)

    def values(j, p, alpha, acc):
        v = vc_ref[0, 0] if j == 0 else vl_ref[0, j - 1]
        pv = jnp.dot(v, p, preferred_element_type=F32)
        return pv if alpha is None else alpha * acc + pv

    def softmax(s, m):
        m_new = jnp.max(s, axis=0, keepdims=True)
        if m is not None:
            m_new = jnp.maximum(m, m_new)
        p = jnp.exp2(s - m_new).astype(BF16)
        return p, m_new, (None if m is None else jnp.exp2(m - m_new))

    def query_tile(qi):
        m = alpha = acc = p = None
        s = s_first_ref[...]
        for j in range(n_k):
            if j + 1 < n_k:
                s_next = scores(qi, j + 1)
            else:
                s_first_ref[...] = scores(jnp.minimum(qi + 1, n_q - 1), 0)
            if j >= 1:
                acc = values(j - 1, p, alpha, acc)
            p, m, alpha = softmax(s, m)
            s = s_next
        acc = values(n_k - 1, p, alpha, acc)
        o_ref[0, 0, qi] = (acc[:dv] / acc[dv:dv + 1] * g_ref[0, 0, qi].astype(F32)).astype(BF16)

    def query_tiles(i, carry):
        for u in range(Q_UNROLL):
            query_tile(i * Q_UNROLL + u)
        return carry

    s_first_ref[...] = scores(0, 0)
    lax.fori_loop(0, n_q // Q_UNROLL, query_tiles, 0)


def _attention(qT, kc, kl, vcT, vlT, gT, *, group, gate_head0, name):
    n_heads = qT.shape[1]
    dv = A_V
    kv_head = lambda b, h: (b, 0, h // group)
    v_head = lambda b, h: (b, 0, h // group, 0)
    return pl.pallas_call(
        _attn_kernel,
        grid=(BATCH, n_heads),
        in_specs=[
            pl.BlockSpec((1, 1, N_Q_TILES, HEAD_PAD, Q_TILE), lambda b, h: (b, h, 0, 0, 0)),
            pl.BlockSpec((1, CTX_LEN, HEAD_PAD), kv_head),
            pl.BlockSpec((1, SEQ, HEAD_PAD), kv_head),
            pl.BlockSpec((1, 1, V_ROWS, TOK_TILE), v_head),
            pl.BlockSpec((1, N_LAT_TILES, V_ROWS, TOK_TILE), v_head),
            pl.BlockSpec((1, 1, N_Q_TILES, dv, Q_TILE), lambda b, h: (b, h + gate_head0, 0, 0, 0)),
        ],
        out_specs=pl.BlockSpec((1, 1, N_Q_TILES, dv, Q_TILE), lambda b, h: (b, h, 0, 0, 0)),
        out_shape=jax.ShapeDtypeStruct((BATCH, n_heads, N_Q_TILES, dv, Q_TILE), BF16),
        scratch_shapes=[pltpu.VMEM((TOK_TILE, Q_TILE), F32)],
        compiler_params=pltpu.CompilerParams(
            dimension_semantics=("arbitrary", "arbitrary"), vmem_limit_bytes=VMEM_LIMIT),
        name=name,
    )(qT, kc, kl, vcT, vlT, gT)


def _post_kernel(ma_ref, mb_ref, x_ref, mod_e_ref, mod_o_ref, woe_ref, lnge_ref, lnbe_ref,
                 wino_ref, vlng_ref, vlnb_ref, ws_ref, bs_ref, woo_ref, lngo_ref, lnbo_ref,
                 out_ref, z_ref):
    b = pl.program_id(0)
    tn = (((0,), (0,)), ((), ()))
    gate_e = mod_e_ref[pl.ds(b, 1), :][:, 2 * D_MODEL:]
    mo = mod_o_ref[pl.ds(b, 1), :]
    shift, scale, gate = mo[:, :D_MODEL], mo[:, D_MODEL:2 * D_MODEL], mo[:, 2 * D_MODEL:]

    def sub_tile(sub):
        tok = slice(sub * POST_SUB, (sub + 1) * POST_SUB)
        ma = ma_ref[0, :, 0, :, tok].reshape(A_WIDTH, POST_SUB)
        mb = mb_ref[0, :, 0, :, tok].reshape(B_WIDTH, POST_SUB)
        y = (lax.dot_general(ma, woe_ref[0:A_WIDTH, :], tn, preferred_element_type=F32)
             + lax.dot_general(mb, woe_ref[A_WIDTH:, :], tn, preferred_element_type=F32))
        yield
        x1 = _layer_norm_rows(DEEPNORM_ALPHA * x_ref[0, tok, :] + gate_e * y,
                              lnge_ref[...], lnbe_ref[...])
        xm = (x1 * (1.0 + scale) + shift).astype(BF16)
        yield
        h_u = jnp.dot(xm, wino_ref[:, :C_WIDTH], preferred_element_type=F32)
        yield
        h_v = jnp.dot(xm, wino_ref[:, C_WIDTH:2 * C_WIDTH], preferred_element_type=F32)
        u = jax.nn.gelu(h_u)
        yield
        h_g = jnp.dot(xm, wino_ref[:, 2 * C_WIDTH:], preferred_element_type=F32)
        v = _layer_norm_rows(jax.nn.gelu(h_v), vlng_ref[...], vlnb_ref[...]).astype(BF16)
        yield
        ug = u * jax.nn.silu(h_g)
        yield
        for n in range(POST_SUB // C_CHUNK):
            rows = slice(n * C_CHUNK, (n + 1) * C_CHUNK)
            zrows = slice(sub * POST_SUB + n * C_CHUNK, sub * POST_SUB + (n + 1) * C_CHUNK)
            for gi in range(C_GROUPS):
                cols = slice(gi * C_GROUP_W, (gi + 1) * C_GROUP_W)
                mixed = jnp.dot(ws_ref[gi], v[rows, cols], preferred_element_type=F32) + bs_ref[gi]
                z_ref[zrows, cols] = (ug[rows, cols] * mixed).astype(BF16)
        y2 = jnp.dot(z_ref[tok, :], woo_ref[...], preferred_element_type=F32)
        yield
        out_ref[0, tok, :] = _layer_norm_rows(DEEPNORM_ALPHA * x1 + gate * y2,
                                              lngo_ref[...], lnbo_ref[...])

    _run_staggered(sub_tile(s) for s in range(x_ref.shape[1] // POST_SUB))


def _post(ma, mb, x, mod_e, mod_o, woe, lnge, lnbe, wino, vlng, vlnb, ws, bs, woo, lngo, lnbo):
    tt = POST_TILE
    assert tt == Q_TILE
    full2 = lambda b, t: (0, 0)
    full3 = lambda b, t: (0, 0, 0)
    in_specs = [
        pl.BlockSpec((1, A_HEADS, 1, A_V, tt), lambda b, t: (b, 0, t, 0, 0)),
        pl.BlockSpec((1, B_HEADS, 1, B_HD, tt), lambda b, t: (b, 0, t, 0, 0)),
        pl.BlockSpec((1, tt, D_MODEL), lambda b, t: (b, t, 0)),
        pl.BlockSpec(mod_e.shape, full2),
        pl.BlockSpec(mod_o.shape, full2),
        pl.BlockSpec(woe.shape, full2),
        pl.BlockSpec(lnge.shape, full2),
        pl.BlockSpec(lnbe.shape, full2),
        pl.BlockSpec(wino.shape, full2),
        pl.BlockSpec(vlng.shape, full2),
        pl.BlockSpec(vlnb.shape, full2),
        pl.BlockSpec(ws.shape, full3),
        pl.BlockSpec(bs.shape, full3),
        pl.BlockSpec(woo.shape, full2),
        pl.BlockSpec(lngo.shape, full2),
        pl.BlockSpec(lnbo.shape, full2),
    ]
    return pl.pallas_call(
        _post_kernel,
        grid=(BATCH, SEQ // tt),
        in_specs=in_specs,
        out_specs=pl.BlockSpec((1, tt, D_MODEL), lambda b, t: (b, t, 0)),
        out_shape=jax.ShapeDtypeStruct((BATCH, SEQ, D_MODEL), F32),
        scratch_shapes=[pltpu.VMEM((tt, C_WIDTH), BF16)],
        compiler_params=pltpu.CompilerParams(
            dimension_semantics=("arbitrary", "arbitrary"), vmem_limit_bytes=VMEM_LIMIT),
        name="out_proj_odd_layer",
    )(ma, mb, x, mod_e, mod_o, woe, lnge, lnbe, wino, vlng, vlnb, ws, bs, woo, lngo, lnbo)


def _rope_table(d_rot):
    rows = SEQ // GRID_W
    row = jnp.repeat(jnp.arange(rows, dtype=F32), GRID_W)
    col = jnp.tile(jnp.arange(GRID_W, dtype=F32), rows)
    d_axis = d_rot // 2
    inv_freq = ROPE_THETA ** (-jnp.arange(0, d_axis, 2, dtype=F32) / d_axis)
    ang_r = row[None, :] * inv_freq[:, None]
    ang_c = col[None, :] * inv_freq[:, None]
    return jnp.concatenate([jnp.cos(ang_r), jnp.sin(ang_r), jnp.cos(ang_c), jnp.sin(ang_c)], axis=0)


def kernel(x, c, ctx, c_ctx, e_w_mod, e_b_mod, e_w_in, e_a_q_norm, e_a_kv_norm, e_a_w_uq, e_a_w_ukv, e_b_q_norm, e_b_k_norm, e_w_out, e_ln_g, e_ln_b, o_w_mod, o_b_mod, o_w_in, o_v_ln_g, o_v_ln_b, o_w_s, o_b_s, o_w_out, o_ln_g, o_ln_b):
    cond = jnp.concatenate(
        [c, c_ctx[None, :], jnp.zeros((MOD_ROWS - BATCH - 1, D_MODEL), F32)], axis=0)
    mod_e = _modulation(cond, e_w_mod[0], e_b_mod[0])
    mod_o = _modulation(cond, o_w_mod[0], o_b_mod[0])

    winT = e_w_in[0].T.astype(BF16)
    wuqT = e_a_w_uq[0].T.astype(BF16)
    wukv = e_a_w_ukv[0].reshape(A_KV_RANK, A_HEADS, A_NOPE + A_V)
    wukT = wukv[:, :, :A_NOPE].reshape(A_KV_RANK, A_HEADS * A_NOPE).T.astype(BF16)
    wuvT = wukv[:, :, A_NOPE:].reshape(A_KV_RANK, A_HEADS * A_V).T.astype(BF16)
    col = lambda v: v.reshape(-1, 1)
    row = lambda v: v.reshape(1, -1)
    proj_params = (mod_e, winT, col(e_a_q_norm[0]), col(e_a_kv_norm[0]), wuqT, wukT, wuvT,
                   col(e_b_q_norm[0]), col(e_b_k_norm[0]), _rope_table(A_ROPE), _rope_table(B_HD))

    kac, vac, kbc, vbc = _even_projection(ctx, *proj_params, latent=False)
    kal, val, kbl, vbl, qa, qb, gates = _even_projection(x, *proj_params, latent=True)

    ma = _attention(qa, kac, kal, vac, val, gates, group=1, gate_head0=0, name="attention_mla")
    mb = _attention(qb, kbc, kbl, vbc, vbl, gates, group=B_GROUP, gate_head0=A_HEADS,
                    name="attention_gqa")

    bs = jnp.broadcast_to(o_b_s[0][:, :, None], (C_GROUPS, C_CHUNK, C_GROUP_W))
    return _post(ma, mb, x, mod_e, mod_o, e_w_out[0].astype(BF16), row(e_ln_g[0]), row(e_ln_b[0]),
                 o_w_in[0].astype(BF16), row(o_v_ln_g[0]), row(o_v_ln_b[0]),
                 o_w_s[0].astype(BF16), bs, o_w_out[0].astype(BF16), row(o_ln_g[0]), row(o_ln_b[0]))
```

```python
import functools

import jax
import jax.numpy as jnp
from jax import lax
from jax.experimental import pallas as pl
from jax.experimental.pallas import tpu as pltpu

D_MODEL = 1024
BATCH = 4
SEQ = 8192
DEPTH = 2
GRID_W = 64
CTX_LEN = 256
ROPE_THETA = 10000.0
EPS = 1e-6

A_HEADS = 8
A_NOPE = 64
A_ROPE = 32
A_V = 64
A_Q_RANK = 384
A_KV_RANK = 256
A_WIDTH = A_HEADS * A_V
A_SCALE = (A_NOPE + A_ROPE) ** -0.5

B_HEADS = 8
B_KV_HEADS = 2
B_GROUP = B_HEADS // B_KV_HEADS
B_HD = 64
B_WIDTH = B_HEADS * B_HD
B_SCALE = B_HD ** -0.5

EVEN_SPLITS = [A_Q_RANK, A_KV_RANK, A_ROPE, A_WIDTH,
               B_HEADS * B_HD, B_KV_HEADS * B_HD, B_KV_HEADS * B_HD, B_WIDTH]
EVEN_IN = sum(EVEN_SPLITS)
ROW_CKV, ROW_KR, ROW_GA, ROW_QB, ROW_KB, ROW_VB, ROW_GB = (
    sum(EVEN_SPLITS[:i]) for i in range(1, len(EVEN_SPLITS)))

C_CHUNK = 128
C_GROUPS = 8
C_WIDTH = 1024
C_GROUP_W = C_WIDTH // C_GROUPS
ODD_IN = 3 * C_WIDTH

DEEPNORM_ALPHA = (2 * DEPTH) ** 0.25

HEAD_PAD = 128
MOD_ROWS = 8
CTX_MOD_ROW = BATCH

TOK_TILE = 256
PROJ_SUBS = 2
N_LAT_TILES = SEQ // TOK_TILE
N_KEY_TILES = (CTX_LEN + SEQ) // TOK_TILE
assert CTX_LEN == TOK_TILE
Q_TILE = 512
N_Q_TILES = SEQ // Q_TILE
Q_UNROLL = 2
V_ROWS = 80
LOG2E = 1.4426950408889634
POST_TILE = 512
POST_SUB = 256
MOD_COL_TILE = 768
VMEM_LIMIT = 56 * 1024 * 1024
MAX_JUMP = 64.0

F32 = jnp.float32
BF16 = jnp.bfloat16


def _layer_norm_rows(xf, g, b):
    mu = jnp.mean(xf, axis=-1, keepdims=True)
    var = jnp.mean(jnp.square(xf - mu), axis=-1, keepdims=True)
    return (xf - mu) * lax.rsqrt(var + EPS) * g + b


def _run_staggered(stage_generators):
    pending = list(stage_generators)
    started = 0
    while pending:
        started = min(started + 1, len(pending))
        for gen in list(pending[:started]):
            try:
                next(gen)
            except StopIteration:
                pending.remove(gen)
                started -= 1


def _mod_kernel(c_ref, w_ref, b_ref, o_ref):
    a = jax.nn.silu(c_ref[...])
    o_ref[...] = jnp.dot(a, w_ref[...], preferred_element_type=F32,
                         precision=lax.Precision.HIGHEST) + b_ref[...]


def _modulation(cond, w_mod, b_mod):
    n = w_mod.shape[1]
    return pl.pallas_call(
        _mod_kernel,
        grid=(n // MOD_COL_TILE,),
        in_specs=[pl.BlockSpec((MOD_ROWS, D_MODEL), lambda j: (0, 0)),
                  pl.BlockSpec((D_MODEL, MOD_COL_TILE), lambda j: (0, j)),
                  pl.BlockSpec((1, MOD_COL_TILE), lambda j: (0, j))],
        out_specs=pl.BlockSpec((MOD_ROWS, MOD_COL_TILE), lambda j: (0, j)),
        out_shape=jax.ShapeDtypeStruct((MOD_ROWS, n), F32),
        name="modulation",
    )(cond, w_mod, b_mod.reshape(1, n))


def _rms_cols(v, g):
    ms = jnp.mean(v * v, axis=0, keepdims=True)
    return v * lax.rsqrt(ms + EPS) * g


def _rope_cols(p, tab):
    if tab is None:
        return p
    q = p.shape[0] // 4
    x1r, x2r, x1c, x2c = p[0:q], p[q:2 * q], p[2 * q:3 * q], p[3 * q:4 * q]
    cr, sr, cc, sc = tab[0:q], tab[q:2 * q], tab[2 * q:3 * q], tab[3 * q:4 * q]
    return jnp.concatenate([x1r * cr - x2r * sr, x2r * cr + x1r * sr,
                            x1c * cc - x2c * sc, x2c * cc + x1c * sc], axis=0)


def _even_proj_kernel(*refs, latent):
    if latent:
        (x_ref, mod_ref, winT_ref, gq_ref, gkv_ref, wuqT_ref, wukT_ref, wuvT_ref, gbq_ref, gbk_ref,
         tab_a_ref, tab_b_ref, ka_ref, va_ref, kb_ref, vb_ref, qa_ref, qb_ref, g_ref) = refs
        mod_row = pl.program_id(0)
    else:
        (x_ref, mod_ref, winT_ref, gkv_ref, wukT_ref, wuvT_ref, gbk_ref,
         ka_ref, va_ref, kb_ref, vb_ref) = refs
        mod_row = CTX_MOD_ROW
    mod = mod_ref[pl.ds(mod_row, 1), :]
    shift, scale = mod[:, :D_MODEL], mod[:, D_MODEL:2 * D_MODEL]
    tt = TOK_TILE
    zpad_a = jnp.zeros((HEAD_PAD - A_NOPE - A_ROPE, tt), F32)
    zpad_b = jnp.zeros((HEAD_PAD - B_HD, tt), F32)
    ones_pad = jnp.where(lax.broadcasted_iota(jnp.int32, (V_ROWS - A_V, tt), 0) == 0, 1.0, 0.0)
    dq = A_NOPE + A_ROPE

    def sub_tile(sub):
        tok = slice(sub * tt, (sub + 1) * tt)
        xm = (x_ref[0, tok, :] * (1.0 + scale) + shift).astype(BF16)
        yield

        def project(lo, hi):
            return lax.dot_general(winT_ref[lo:hi, :], xm, (((1,), (1,)), ((), ())),
                                   preferred_element_type=F32)

        cq_ckv_kr = project(0, ROW_GA)
        cq, ckv, kr = cq_ckv_kr[:ROW_CKV], cq_ckv_kr[ROW_CKV:ROW_KR], cq_ckv_kr[ROW_KR:]
        yield
        qb_kb_vb = project(ROW_QB if latent else ROW_KB, ROW_GB)
        kb, vb = qb_kb_vb[-2 * B_KV_HEADS * B_HD:-B_KV_HEADS * B_HD], qb_kb_vb[-B_KV_HEADS * B_HD:]
        qb = qb_kb_vb[:B_WIDTH] if latent else None
        tab_a = tab_a_ref[:, tok] if latent else None
        tab_b = tab_b_ref[:, tok] if latent else None
        ckvn = _rms_cols(ckv, gkv_ref[...]).astype(BF16)
        if latent:
            cqn = _rms_cols(cq, gq_ref[...]).astype(BF16)
        yield
        kn = jnp.dot(wukT_ref[...], ckvn, preferred_element_type=F32)
        va = jnp.dot(wuvT_ref[...], ckvn, preferred_element_type=F32)
        if latent:
            qa = jnp.dot(wuqT_ref[...], cqn, preferred_element_type=F32)
        gbk = gbk_ref[...]
        kb_heads, vb_heads = [], []
        for j in range(B_KV_HEADS):
            kb_heads += [_rope_cols(_rms_cols(kb[B_HD * j:B_HD * (j + 1)], gbk), tab_b), zpad_b]
            vb_heads += [vb[B_HD * j:B_HD * (j + 1)], ones_pad]
        kb_ref[0, tok, :] = jnp.concatenate(kb_heads, axis=0).T.astype(BF16)
        vb_ref[0, sub] = jnp.concatenate(vb_heads, axis=0).astype(BF16)
        if latent:
            gbq = gbq_ref[...]
            for h in range(B_HEADS):
                qh = jnp.concatenate(
                    [_rope_cols(_rms_cols(qb[B_HD * h:B_HD * (h + 1)], gbq), tab_b), zpad_b], axis=0)
                qb_ref[0, h, 0, :, tok] = (qh * (B_SCALE * LOG2E)).astype(BF16)
        yield
        if latent:
            ga = project(ROW_GA, ROW_QB)
            gb = project(ROW_GB, EVEN_IN)
        k_pe = _rope_cols(kr, tab_a)
        ka_heads, va_heads = [], []
        for h in range(A_HEADS):
            ka_heads += [kn[A_NOPE * h:A_NOPE * (h + 1)], k_pe, zpad_a]
            va_heads += [va[A_V * h:A_V * (h + 1)], ones_pad]
        ka_ref[0, tok, :] = jnp.concatenate(ka_heads, axis=0).T.astype(BF16)
        va_ref[0, sub] = jnp.concatenate(va_heads, axis=0).astype(BF16)
        if latent:
            for h in range(A_HEADS):
                qh = jnp.concatenate(
                    [qa[dq * h:dq * h + A_NOPE],
                     _rope_cols(qa[dq * h + A_NOPE:dq * (h + 1)], tab_a), zpad_a], axis=0)
                qa_ref[0, h, 0, :, tok] = (qh * (A_SCALE * LOG2E)).astype(BF16)
            yield
            for h in range(A_HEADS):
                g_ref[0, h, 0, :, tok] = jax.nn.silu(ga[A_V * h:A_V * (h + 1)]).astype(BF16)
            for h in range(B_HEADS):
                g_ref[0, A_HEADS + h, 0, :, tok] = jax.nn.silu(
                    gb[B_HD * h:B_HD * (h + 1)]).astype(BF16)

    _run_staggered(sub_tile(s) for s in range(x_ref.shape[1] // tt))


def _even_projection(x, mod, winT, gq, gkv, wuqT, wukT, wuvT, gbq, gbk, tab_a, tab_b, *, latent):
    n_tok = x.shape[1]
    blk = PROJ_SUBS * TOK_TILE if latent else CTX_LEN
    subs = blk // TOK_TILE
    assert Q_TILE == blk or not latent
    full2 = lambda b, t: (0, 0)
    kv_shapes = (
        jax.ShapeDtypeStruct((BATCH, n_tok, A_HEADS * HEAD_PAD), BF16),
        jax.ShapeDtypeStruct((BATCH, n_tok // TOK_TILE, A_HEADS * V_ROWS, TOK_TILE), BF16),
        jax.ShapeDtypeStruct((BATCH, n_tok, B_KV_HEADS * HEAD_PAD), BF16),
        jax.ShapeDtypeStruct((BATCH, n_tok // TOK_TILE, B_KV_HEADS * V_ROWS, TOK_TILE), BF16),
    )
    kv_specs = (
        pl.BlockSpec((1, blk, A_HEADS * HEAD_PAD), lambda b, t: (b, t, 0)),
        pl.BlockSpec((1, subs, A_HEADS * V_ROWS, TOK_TILE), lambda b, t: (b, t, 0, 0)),
        pl.BlockSpec((1, blk, B_KV_HEADS * HEAD_PAD), lambda b, t: (b, t, 0)),
        pl.BlockSpec((1, subs, B_KV_HEADS * V_ROWS, TOK_TILE), lambda b, t: (b, t, 0, 0)),
    )
    x_spec = pl.BlockSpec((1, blk, D_MODEL), lambda b, t: (b, t, 0))
    whole = lambda a: pl.BlockSpec(a.shape, full2)
    if latent:
        q_tile = lambda b, t: (b, 0, t, 0, 0)
        operands = (x, mod, winT, gq, gkv, wuqT, wukT, wuvT, gbq, gbk, tab_a, tab_b)
        in_specs = [x_spec] + [whole(a) for a in operands[1:10]] + [
            pl.BlockSpec((A_ROPE, blk), lambda b, t: (0, t)),
            pl.BlockSpec((B_HD, blk), lambda b, t: (0, t))]
        out_shape = kv_shapes + (
            jax.ShapeDtypeStruct((BATCH, A_HEADS, N_Q_TILES, HEAD_PAD, Q_TILE), BF16),
            jax.ShapeDtypeStruct((BATCH, B_HEADS, N_Q_TILES, HEAD_PAD, Q_TILE), BF16),
            jax.ShapeDtypeStruct((BATCH, A_HEADS + B_HEADS, N_Q_TILES, A_V, Q_TILE), BF16),
        )
        out_specs = kv_specs + (
            pl.BlockSpec((1, A_HEADS, 1, HEAD_PAD, Q_TILE), q_tile),
            pl.BlockSpec((1, B_HEADS, 1, HEAD_PAD, Q_TILE), q_tile),
            pl.BlockSpec((1, A_HEADS + B_HEADS, 1, A_V, Q_TILE), q_tile),
        )
    else:
        operands = (x, mod, winT, gkv, wukT, wuvT, gbk)
        in_specs = [x_spec] + [whole(a) for a in operands[1:]]
        out_shape, out_specs = kv_shapes, kv_specs
    return pl.pallas_call(
        functools.partial(_even_proj_kernel, latent=latent),
        grid=(BATCH, n_tok // blk),
        in_specs=in_specs,
        out_specs=out_specs,
        out_shape=out_shape,
        compiler_params=pltpu.CompilerParams(
            dimension_semantics=("arbitrary", "arbitrary"), vmem_limit_bytes=VMEM_LIMIT),
        name="even_projection_latent" if latent else "even_projection_context",
    )(*operands)


def _attn_kernel(q_ref, kc_ref, kl_ref, vc_ref, vl_ref, g_ref, o_ref, s_first_ref):
    n_q = q_ref.shape[2]
    tq = q_ref.shape[4]
    dv = o_ref.shape[3]
    n_k = N_KEY_TILES

    def scores(qi, j):
        k = kc_ref[0] if j == 0 else kl_ref[0, (j - 1) * TOK_TILE:j * TOK_TILE, :]
        return jnp.dot(k, q_ref[0, 0, qi], preferred_element_type=F32)

    def finish(qi, acc):
        o_ref[0, 0, qi] = (acc[:dv] / acc[dv:dv + 1] * g_ref[0, 0, qi].astype(F32)).astype(BF16)

    def fast_tile(qi):
        acc = ref = p = down = None
        jump = jnp.zeros((1, tq), F32)
        s = s_first_ref[...]
        for j in range(n_k + 1):
            if j + 1 < n_k:
                s_next = scores(qi, j + 1)
            elif j + 1 == n_k:
                s_first_ref[...] = scores(jnp.minimum(qi + 1, n_q - 1), 0)
            if j >= 1:
                v = vc_ref[0, 0] if j == 1 else vl_ref[0, j - 2]
                pv = jnp.dot(v, p, preferred_element_type=F32)
                acc = pv if acc is None else (acc + pv) * down
            if j == 0:
                ref = jnp.max(s, axis=0, keepdims=True)
                p = jnp.exp2(s - ref).astype(BF16)
            elif j < n_k:
                p = jnp.exp2(s - ref).astype(BF16)
                ref_new = jnp.maximum(ref, jnp.max(s, axis=0, keepdims=True))
                rise = ref_new - ref
                down = jnp.exp2(-rise)
                jump = jnp.maximum(jump, rise)
                ref = ref_new
            s = s_next
        finish(qi, acc)
        return jump

    def exact_tile(qi):
        q = q_ref[0, 0, qi]
        s = jnp.dot(kc_ref[0], q, preferred_element_type=F32)
        m = jnp.max(s, axis=0, keepdims=True)
        acc = jnp.dot(vc_ref[0, 0], jnp.exp2(s - m).astype(BF16), preferred_element_type=F32)

        def key_step(j, carry):
            m, acc = carry
            k = kl_ref[0, pl.ds(pl.multiple_of(j * TOK_TILE, TOK_TILE), TOK_TILE), :]
            s = jnp.dot(k, q, preferred_element_type=F32)
            m_new = jnp.maximum(m, jnp.max(s, axis=0, keepdims=True))
            pv = jnp.dot(vl_ref[0, j], jnp.exp2(s - m_new).astype(BF16),
                         preferred_element_type=F32)
            return m_new, jnp.exp2(m - m_new) * acc + pv

        _, acc = lax.fori_loop(0, N_LAT_TILES, key_step, (m, acc))
        finish(qi, acc)

    def query_tiles(i, carry):
        jump = fast_tile(i * Q_UNROLL)
        for u in range(1, Q_UNROLL):
            jump = jnp.maximum(jump, fast_tile(i * Q_UNROLL + u))

        @pl.when(jnp.max(jump) > MAX_JUMP)
        def _():
            for u in range(Q_UNROLL):
                exact_tile(i * Q_UNROLL + u)

        return carry

    s_first_ref[...] = scores(0, 0)
    lax.fori_loop(0, n_q // Q_UNROLL, query_tiles, 0)


def _attention(qT, kc, kl, vcT, vlT, gT, *, group, gate_head0, name):
    n_heads = qT.shape[1]
    dv = A_V
    kv_head = lambda b, h: (b, 0, h // group)
    v_head = lambda b, h: (b, 0, h // group, 0)
    return pl.pallas_call(
        _attn_kernel,
        grid=(BATCH, n_heads),
        in_specs=[
            pl.BlockSpec((1, 1, N_Q_TILES, HEAD_PAD, Q_TILE), lambda b, h: (b, h, 0, 0, 0)),
            pl.BlockSpec((1, CTX_LEN, HEAD_PAD), kv_head),
            pl.BlockSpec((1, SEQ, HEAD_PAD), kv_head),
            pl.BlockSpec((1, 1, V_ROWS, TOK_TILE), v_head),
            pl.BlockSpec((1, N_LAT_TILES, V_ROWS, TOK_TILE), v_head),
            pl.BlockSpec((1, 1, N_Q_TILES, dv, Q_TILE), lambda b, h: (b, h + gate_head0, 0, 0, 0)),
        ],
        out_specs=pl.BlockSpec((1, 1, N_Q_TILES, dv, Q_TILE), lambda b, h: (b, h, 0, 0, 0)),
        out_shape=jax.ShapeDtypeStruct((BATCH, n_heads, N_Q_TILES, dv, Q_TILE), BF16),
        scratch_shapes=[pltpu.VMEM((TOK_TILE, Q_TILE), F32)],
        compiler_params=pltpu.CompilerParams(
            dimension_semantics=("arbitrary", "arbitrary"), vmem_limit_bytes=VMEM_LIMIT),
        name=name,
    )(qT, kc, kl, vcT, vlT, gT)


def _post_kernel(ma_ref, mb_ref, x_ref, mod_e_ref, mod_o_ref, woe_ref, lnge_ref, lnbe_ref,
                 wino_ref, vlng_ref, vlnb_ref, ws_ref, bs_ref, woo_ref, lngo_ref, lnbo_ref,
                 out_ref, z_ref):
    b = pl.program_id(0)
    tn = (((0,), (0,)), ((), ()))
    gate_e = mod_e_ref[pl.ds(b, 1), :][:, 2 * D_MODEL:]
    mo = mod_o_ref[pl.ds(b, 1), :]
    shift, scale, gate = mo[:, :D_MODEL], mo[:, D_MODEL:2 * D_MODEL], mo[:, 2 * D_MODEL:]

    def sub_tile(sub):
        tok = slice(sub * POST_SUB, (sub + 1) * POST_SUB)
        ma = ma_ref[0, :, 0, :, tok].reshape(A_WIDTH, POST_SUB)
        mb = mb_ref[0, :, 0, :, tok].reshape(B_WIDTH, POST_SUB)
        y = (lax.dot_general(ma, woe_ref[0:A_WIDTH, :], tn, preferred_element_type=F32)
             + lax.dot_general(mb, woe_ref[A_WIDTH:, :], tn, preferred_element_type=F32))
        yield
        x1 = _layer_norm_rows(DEEPNORM_ALPHA * x_ref[0, tok, :] + gate_e * y,
                              lnge_ref[...], lnbe_ref[...])
        xm = (x1 * (1.0 + scale) + shift).astype(BF16)
        yield
        h_u = jnp.dot(xm, wino_ref[:, :C_WIDTH], preferred_element_type=F32)
        yield
        h_v = jnp.dot(xm, wino_ref[:, C_WIDTH:2 * C_WIDTH], preferred_element_type=F32)
        u = jax.nn.gelu(h_u)
        yield
        h_g = jnp.dot(xm, wino_ref[:, 2 * C_WIDTH:], preferred_element_type=F32)
        v = _layer_norm_rows(jax.nn.gelu(h_v), vlng_ref[...], vlnb_ref[...]).astype(BF16)
        yield
        ug = u * jax.nn.silu(h_g)
        yield
        for n in range(POST_SUB // C_CHUNK):
            rows = slice(n * C_CHUNK, (n + 1) * C_CHUNK)
            zrows = slice(sub * POST_SUB + n * C_CHUNK, sub * POST_SUB + (n + 1) * C_CHUNK)
            for gi in range(C_GROUPS):
                cols = slice(gi * C_GROUP_W, (gi + 1) * C_GROUP_W)
                mixed = jnp.dot(ws_ref[gi], v[rows, cols], preferred_element_type=F32) + bs_ref[gi]
                z_ref[zrows, cols] = (ug[rows, cols] * mixed).astype(BF16)
        y2 = jnp.dot(z_ref[tok, :], woo_ref[...], preferred_element_type=F32)
        yield
        out_ref[0, tok, :] = _layer_norm_rows(DEEPNORM_ALPHA * x1 + gate * y2,
                                              lngo_ref[...], lnbo_ref[...])

    _run_staggered(sub_tile(s) for s in range(x_ref.shape[1] // POST_SUB))


def _post(ma, mb, x, mod_e, mod_o, woe, lnge, lnbe, wino, vlng, vlnb, ws, bs, woo, lngo, lnbo):
    tt = POST_TILE
    assert tt == Q_TILE
    full2 = lambda b, t: (0, 0)
    full3 = lambda b, t: (0, 0, 0)
    in_specs = [
        pl.BlockSpec((1, A_HEADS, 1, A_V, tt), lambda b, t: (b, 0, t, 0, 0)),
        pl.BlockSpec((1, B_HEADS, 1, B_HD, tt), lambda b, t: (b, 0, t, 0, 0)),
        pl.BlockSpec((1, tt, D_MODEL), lambda b, t: (b, t, 0)),
        pl.BlockSpec(mod_e.shape, full2),
        pl.BlockSpec(mod_o.shape, full2),
        pl.BlockSpec(woe.shape, full2),
        pl.BlockSpec(lnge.shape, full2),
        pl.BlockSpec(lnbe.shape, full2),
        pl.BlockSpec(wino.shape, full2),
        pl.BlockSpec(vlng.shape, full2),
        pl.BlockSpec(vlnb.shape, full2),
        pl.BlockSpec(ws.shape, full3),
        pl.BlockSpec(bs.shape, full3),
        pl.BlockSpec(woo.shape, full2),
        pl.BlockSpec(lngo.shape, full2),
        pl.BlockSpec(lnbo.shape, full2),
    ]
    return pl.pallas_call(
        _post_kernel,
        grid=(BATCH, SEQ // tt),
        in_specs=in_specs,
        out_specs=pl.BlockSpec((1, tt, D_MODEL), lambda b, t: (b, t, 0)),
        out_shape=jax.ShapeDtypeStruct((BATCH, SEQ, D_MODEL), F32),
        scratch_shapes=[pltpu.VMEM((tt, C_WIDTH), BF16)],
        compiler_params=pltpu.CompilerParams(
            dimension_semantics=("arbitrary", "arbitrary"), vmem_limit_bytes=VMEM_LIMIT),
        name="out_proj_odd_layer",
    )(ma, mb, x, mod_e, mod_o, woe, lnge, lnbe, wino, vlng, vlnb, ws, bs, woo, lngo, lnbo)


def _rope_table(d_rot):
    rows = SEQ // GRID_W
    row = jnp.repeat(jnp.arange(rows, dtype=F32), GRID_W)
    col = jnp.tile(jnp.arange(GRID_W, dtype=F32), rows)
    d_axis = d_rot // 2
    inv_freq = ROPE_THETA ** (-jnp.arange(0, d_axis, 2, dtype=F32) / d_axis)
    ang_r = row[None, :] * inv_freq[:, None]
    ang_c = col[None, :] * inv_freq[:, None]
    return jnp.concatenate([jnp.cos(ang_r), jnp.sin(ang_r), jnp.cos(ang_c), jnp.sin(ang_c)], axis=0)


def kernel(x, c, ctx, c_ctx, e_w_mod, e_b_mod, e_w_in, e_a_q_norm, e_a_kv_norm, e_a_w_uq, e_a_w_ukv, e_b_q_norm, e_b_k_norm, e_w_out, e_ln_g, e_ln_b, o_w_mod, o_b_mod, o_w_in, o_v_ln_g, o_v_ln_b, o_w_s, o_b_s, o_w_out, o_ln_g, o_ln_b):
    cond = jnp.concatenate(
        [c, c_ctx[None, :], jnp.zeros((MOD_ROWS - BATCH - 1, D_MODEL), F32)], axis=0)
    mod_e = _modulation(cond, e_w_mod[0], e_b_mod[0])
    mod_o = _modulation(cond, o_w_mod[0], o_b_mod[0])

    winT = e_w_in[0].T.astype(BF16)
    wuqT = e_a_w_uq[0].T.astype(BF16)
    wukv = e_a_w_ukv[0].reshape(A_KV_RANK, A_HEADS, A_NOPE + A_V)
    wukT = wukv[:, :, :A_NOPE].reshape(A_KV_RANK, A_HEADS * A_NOPE).T.astype(BF16)
    wuvT = wukv[:, :, A_NOPE:].reshape(A_KV_RANK, A_HEADS * A_V).T.astype(BF16)
    col = lambda v: v.reshape(-1, 1)
    row = lambda v: v.reshape(1, -1)
    proj_params = (mod_e, winT, col(e_a_q_norm[0]), col(e_a_kv_norm[0]), wuqT, wukT, wuvT,
                   col(e_b_q_norm[0]), col(e_b_k_norm[0]), _rope_table(A_ROPE), _rope_table(B_HD))

    kac, vac, kbc, vbc = _even_projection(ctx, *proj_params, latent=False)
    kal, val, kbl, vbl, qa, qb, gates = _even_projection(x, *proj_params, latent=True)

    ma = _attention(qa, kac, kal, vac, val, gates, group=1, gate_head0=0, name="attention_mla")
    mb = _attention(qb, kbc, kbl, vbc, vbl, gates, group=B_GROUP, gate_head0=A_HEADS,
                    name="attention_gqa")

    bs = jnp.broadcast_to(o_b_s[0][:, :, None], (C_GROUPS, C_CHUNK, C_GROUP_W))
    return _post(ma, mb, x, mod_e, mod_o, e_w_out[0].astype(BF16), row(e_ln_g[0]), row(e_ln_b[0]),
                 o_w_in[0].astype(BF16), row(o_v_ln_g[0]), row(o_v_ln_b[0]),
                 o_w_s[0].astype(BF16), bs, o_w_out[0].astype(BF16), row(o_ln_g[0]), row(o_ln_b[0]))
```

```python
import functools

import jax
import jax.numpy as jnp
from jax import lax
from jax.experimental import pallas as pl
from jax.experimental.pallas import tpu as pltpu

D_MODEL = 1024
BATCH = 4
SEQ = 8192
DEPTH = 2
GRID_W = 64
CTX_LEN = 256
ROPE_THETA = 10000.0
EPS = 1e-6

A_HEADS = 8
A_NOPE = 64
A_ROPE = 32
A_V = 64
A_Q_RANK = 384
A_KV_RANK = 256
A_WIDTH = A_HEADS * A_V
A_SCALE = (A_NOPE + A_ROPE) ** -0.5

B_HEADS = 8
B_KV_HEADS = 2
B_GROUP = B_HEADS // B_KV_HEADS
B_HD = 64
B_WIDTH = B_HEADS * B_HD
B_SCALE = B_HD ** -0.5

EVEN_SPLITS = [A_Q_RANK, A_KV_RANK, A_ROPE, A_WIDTH,
               B_HEADS * B_HD, B_KV_HEADS * B_HD, B_KV_HEADS * B_HD, B_WIDTH]
EVEN_IN = sum(EVEN_SPLITS)
ROW_CKV, ROW_KR, ROW_GA, ROW_QB, ROW_KB, ROW_VB, ROW_GB = (
    sum(EVEN_SPLITS[:i]) for i in range(1, len(EVEN_SPLITS)))

C_CHUNK = 128
C_GROUPS = 8
C_WIDTH = 1024
C_GROUP_W = C_WIDTH // C_GROUPS
ODD_IN = 3 * C_WIDTH

DEEPNORM_ALPHA = (2 * DEPTH) ** 0.25

HEAD_PAD = 128
MOD_ROWS = 8
CTX_MOD_ROW = BATCH

TOK_TILE = 256
PROJ_SUBS = 2
N_LAT_TILES = SEQ // TOK_TILE
N_KEY_TILES = (CTX_LEN + SEQ) // TOK_TILE
assert CTX_LEN == TOK_TILE
Q_TILE = 512
N_Q_TILES = SEQ // Q_TILE
Q_UNROLL = 2
V_ROWS = 80
LOG2E = 1.4426950408889634
POST_TILE = 512
POST_SUB = 256
MOD_COL_TILE = 768
VMEM_LIMIT = 56 * 1024 * 1024
MAX_JUMP = 64.0

F32 = jnp.float32
BF16 = jnp.bfloat16


def _layer_norm_rows(xf, g, b):
    mu = jnp.mean(xf, axis=-1, keepdims=True)
    var = jnp.mean(jnp.square(xf - mu), axis=-1, keepdims=True)
    return (xf - mu) * lax.rsqrt(var + EPS) * g + b


def _run_staggered(stage_generators):
    pending = list(stage_generators)
    started = 0
    while pending:
        started = min(started + 1, len(pending))
        for gen in list(pending[:started]):
            try:
                next(gen)
            except StopIteration:
                pending.remove(gen)
                started -= 1


def _mod_kernel(c_ref, w_ref, b_ref, o_ref):
    a = jax.nn.silu(c_ref[...])
    o_ref[...] = jnp.dot(a, w_ref[...], preferred_element_type=F32,
                         precision=lax.Precision.HIGHEST) + b_ref[...]


def _modulation(cond, w_mod, b_mod):
    n = w_mod.shape[1]
    return pl.pallas_call(
        _mod_kernel,
        grid=(n // MOD_COL_TILE,),
        in_specs=[pl.BlockSpec((MOD_ROWS, D_MODEL), lambda j: (0, 0)),
                  pl.BlockSpec((D_MODEL, MOD_COL_TILE), lambda j: (0, j)),
                  pl.BlockSpec((1, MOD_COL_TILE), lambda j: (0, j))],
        out_specs=pl.BlockSpec((MOD_ROWS, MOD_COL_TILE), lambda j: (0, j)),
        out_shape=jax.ShapeDtypeStruct((MOD_ROWS, n), F32),
        name="modulation",
    )(cond, w_mod, b_mod.reshape(1, n))


def _rms_cols(v, g):
    ms = jnp.mean(v * v, axis=0, keepdims=True)
    return v * lax.rsqrt(ms + EPS) * g


def _rope_cols(p, tab):
    if tab is None:
        return p
    q = p.shape[0] // 4
    x1r, x2r, x1c, x2c = p[0:q], p[q:2 * q], p[2 * q:3 * q], p[3 * q:4 * q]
    cr, sr, cc, sc = tab[0:q], tab[q:2 * q], tab[2 * q:3 * q], tab[3 * q:4 * q]
    return jnp.concatenate([x1r * cr - x2r * sr, x2r * cr + x1r * sr,
                            x1c * cc - x2c * sc, x2c * cc + x1c * sc], axis=0)


def _even_proj_kernel(*refs, latent):
    if latent:
        (x_ref, mod_ref, winT_ref, gq_ref, gkv_ref, wuqT_ref, wukT_ref, wuvT_ref, gbq_ref, gbk_ref,
         tab_a_ref, tab_b_ref, ka_ref, va_ref, kb_ref, vb_ref, qa_ref, qb_ref, g_ref) = refs
        mod_row = pl.program_id(0)
    else:
        (x_ref, mod_ref, winT_ref, gkv_ref, wukT_ref, wuvT_ref, gbk_ref,
         ka_ref, va_ref, kb_ref, vb_ref) = refs
        mod_row = CTX_MOD_ROW
    mod = mod_ref[pl.ds(mod_row, 1), :]
    shift, scale = mod[:, :D_MODEL], mod[:, D_MODEL:2 * D_MODEL]
    tt = TOK_TILE
    zpad_a = jnp.zeros((HEAD_PAD - A_NOPE - A_ROPE, tt), F32)
    zpad_b = jnp.zeros((HEAD_PAD - B_HD, tt), F32)
    ones_pad = jnp.where(lax.broadcasted_iota(jnp.int32, (V_ROWS - A_V, tt), 0) == 0, 1.0, 0.0)
    dq = A_NOPE + A_ROPE

    def sub_tile(sub):
        tok = slice(sub * tt, (sub + 1) * tt)
        xm = (x_ref[0, tok, :] * (1.0 + scale) + shift).astype(BF16)
        yield

        def project(lo, hi):
            return lax.dot_general(winT_ref[lo:hi, :], xm, (((1,), (1,)), ((), ())),
                                   preferred_element_type=F32)

        cq_ckv_kr = project(0, ROW_GA)
        cq, ckv, kr = cq_ckv_kr[:ROW_CKV], cq_ckv_kr[ROW_CKV:ROW_KR], cq_ckv_kr[ROW_KR:]
        yield
        qb_kb_vb = project(ROW_QB if latent else ROW_KB, ROW_GB)
        kb, vb = qb_kb_vb[-2 * B_KV_HEADS * B_HD:-B_KV_HEADS * B_HD], qb_kb_vb[-B_KV_HEADS * B_HD:]
        qb = qb_kb_vb[:B_WIDTH] if latent else None
        tab_a = tab_a_ref[:, tok] if latent else None
        tab_b = tab_b_ref[:, tok] if latent else None
        ckvn = _rms_cols(ckv, gkv_ref[...]).astype(BF16)
        if latent:
            cqn = _rms_cols(cq, gq_ref[...]).astype(BF16)
        yield
        kn = jnp.dot(wukT_ref[...], ckvn, preferred_element_type=F32)
        va = jnp.dot(wuvT_ref[...], ckvn, preferred_element_type=F32)
        if latent:
            qa = jnp.dot(wuqT_ref[...], cqn, preferred_element_type=F32)
        gbk = gbk_ref[...]
        kb_heads, vb_heads = [], []
        for j in range(B_KV_HEADS):
            kb_heads += [_rope_cols(_rms_cols(kb[B_HD * j:B_HD * (j + 1)], gbk), tab_b), zpad_b]
            vb_heads += [vb[B_HD * j:B_HD * (j + 1)], ones_pad]
        kb_ref[0, tok, :] = jnp.concatenate(kb_heads, axis=0).T.astype(BF16)
        vb_ref[0, sub] = jnp.concatenate(vb_heads, axis=0).astype(BF16)
        if latent:
            gbq = gbq_ref[...]
            for h in range(B_HEADS):
                qh = jnp.concatenate(
                    [_rope_cols(_rms_cols(qb[B_HD * h:B_HD * (h + 1)], gbq), tab_b), zpad_b], axis=0)
                qb_ref[0, h, 0, :, tok] = (qh * (B_SCALE * LOG2E)).astype(BF16)
        yield
        if latent:
            ga = project(ROW_GA, ROW_QB)
            gb = project(ROW_GB, EVEN_IN)
        k_pe = _rope_cols(kr, tab_a)
        ka_heads, va_heads = [], []
        for h in range(A_HEADS):
            ka_heads += [kn[A_NOPE * h:A_NOPE * (h + 1)], k_pe, zpad_a]
            va_heads += [va[A_V * h:A_V * (h + 1)], ones_pad]
        ka_ref[0, tok, :] = jnp.concatenate(ka_heads, axis=0).T.astype(BF16)
        va_ref[0, sub] = jnp.concatenate(va_heads, axis=0).astype(BF16)
        if latent:
            for h in range(A_HEADS):
                qh = jnp.concatenate(
                    [qa[dq * h:dq * h + A_NOPE],
                     _rope_cols(qa[dq * h + A_NOPE:dq * (h + 1)], tab_a), zpad_a], axis=0)
                qa_ref[0, h, 0, :, tok] = (qh * (A_SCALE * LOG2E)).astype(BF16)
            yield
            for h in range(A_HEADS):
                g_ref[0, h, 0, :, tok] = jax.nn.silu(ga[A_V * h:A_V * (h + 1)]).astype(BF16)
            for h in range(B_HEADS):
                g_ref[0, A_HEADS + h, 0, :, tok] = jax.nn.silu(
                    gb[B_HD * h:B_HD * (h + 1)]).astype(BF16)

    _run_staggered(sub_tile(s) for s in range(x_ref.shape[1] // tt))


def _even_projection(x, mod, winT, gq, gkv, wuqT, wukT, wuvT, gbq, gbk, tab_a, tab_b, *, latent):
    n_tok = x.shape[1]
    blk = PROJ_SUBS * TOK_TILE if latent else CTX_LEN
    subs = blk // TOK_TILE
    assert Q_TILE == blk or not latent
    full2 = lambda b, t: (0, 0)
    kv_shapes = (
        jax.ShapeDtypeStruct((BATCH, n_tok, A_HEADS * HEAD_PAD), BF16),
        jax.ShapeDtypeStruct((BATCH, n_tok // TOK_TILE, A_HEADS * V_ROWS, TOK_TILE), BF16),
        jax.ShapeDtypeStruct((BATCH, n_tok, B_KV_HEADS * HEAD_PAD), BF16),
        jax.ShapeDtypeStruct((BATCH, n_tok // TOK_TILE, B_KV_HEADS * V_ROWS, TOK_TILE), BF16),
    )
    kv_specs = (
        pl.BlockSpec((1, blk, A_HEADS * HEAD_PAD), lambda b, t: (b, t, 0)),
        pl.BlockSpec((1, subs, A_HEADS * V_ROWS, TOK_TILE), lambda b, t: (b, t, 0, 0)),
        pl.BlockSpec((1, blk, B_KV_HEADS * HEAD_PAD), lambda b, t: (b, t, 0)),
        pl.BlockSpec((1, subs, B_KV_HEADS * V_ROWS, TOK_TILE), lambda b, t: (b, t, 0, 0)),
    )
    x_spec = pl.BlockSpec((1, blk, D_MODEL), lambda b, t: (b, t, 0))
    whole = lambda a: pl.BlockSpec(a.shape, full2)
    if latent:
        q_tile = lambda b, t: (b, 0, t, 0, 0)
        operands = (x, mod, winT, gq, gkv, wuqT, wukT, wuvT, gbq, gbk, tab_a, tab_b)
        in_specs = [x_spec] + [whole(a) for a in operands[1:10]] + [
            pl.BlockSpec((A_ROPE, blk), lambda b, t: (0, t)),
            pl.BlockSpec((B_HD, blk), lambda b, t: (0, t))]
        out_shape = kv_shapes + (
            jax.ShapeDtypeStruct((BATCH, A_HEADS, N_Q_TILES, HEAD_PAD, Q_TILE), BF16),
            jax.ShapeDtypeStruct((BATCH, B_HEADS, N_Q_TILES, HEAD_PAD, Q_TILE), BF16),
            jax.ShapeDtypeStruct((BATCH, A_HEADS + B_HEADS, N_Q_TILES, A_V, Q_TILE), BF16),
        )
        out_specs = kv_specs + (
            pl.BlockSpec((1, A_HEADS, 1, HEAD_PAD, Q_TILE), q_tile),
            pl.BlockSpec((1, B_HEADS, 1, HEAD_PAD, Q_TILE), q_tile),
            pl.BlockSpec((1, A_HEADS + B_HEADS, 1, A_V, Q_TILE), q_tile),
        )
    else:
        operands = (x, mod, winT, gkv, wukT, wuvT, gbk)
        in_specs = [x_spec] + [whole(a) for a in operands[1:]]
        out_shape, out_specs = kv_shapes, kv_specs
    return pl.pallas_call(
        functools.partial(_even_proj_kernel, latent=latent),
        grid=(BATCH, n_tok // blk),
        in_specs=in_specs,
        out_specs=out_specs,
        out_shape=out_shape,
        compiler_params=pltpu.CompilerParams(
            dimension_semantics=("arbitrary", "arbitrary"), vmem_limit_bytes=VMEM_LIMIT),
        name="even_projection_latent" if latent else "even_projection_context",
    )(*operands)


def _attn_kernel(q_ref, kc_ref, kl_ref, vc_ref, vl_ref, g_ref, o_ref, s_first_ref):
    n_q = q_ref.shape[2]
    tq = q_ref.shape[4]
    dv = o_ref.shape[3]
    n_k = N_KEY_TILES

    def scores(qi, j):
        k = kc_ref[0] if j == 0 else kl_ref[0, (j - 1) * TOK_TILE:j * TOK_TILE, :]
        return jnp.dot(k, q_ref[0, 0, qi], preferred_element_type=F32)

    def finish(qi, acc):
        o_ref[0, 0, qi] = (acc[:dv] / acc[dv:dv + 1] * g_ref[0, 0, qi].astype(F32)).astype(BF16)

    def fast_tile(qi):
        acc = ref = p = down = None
        jump = jnp.zeros((1, tq), F32)
        s = s_first_ref[0]
        q_next = jnp.minimum(qi + 1, n_q - 1)
        for j in range(n_k + 1):
            if j == 0:
                s_next = s_first_ref[1]
            elif j + 1 < n_k:
                s_next = scores(qi, j + 1)
            elif j + 1 == n_k:
                s_first_ref[0] = scores(q_next, 0)
            else:
                s_first_ref[1] = scores(q_next, 1)
            if j >= 1:
                v = vc_ref[0, 0] if j == 1 else vl_ref[0, j - 2]
                pv = jnp.dot(v, p, preferred_element_type=F32)
                acc = pv if acc is None else (acc + pv) * down
            if j == 0:
                ref = jnp.max(s, axis=0, keepdims=True)
                p = jnp.exp2(s - ref).astype(BF16)
            elif j < n_k:
                p = jnp.exp2(s - ref).astype(BF16)
                ref_new = jnp.maximum(ref, jnp.max(s, axis=0, keepdims=True))
                rise = ref_new - ref
                down = jnp.exp2(-rise)
                jump = jnp.maximum(jump, rise)
                ref = ref_new
            s = s_next
        finish(qi, acc)
        return jump

    def exact_tile(qi):
        q = q_ref[0, 0, qi]
        s = jnp.dot(kc_ref[0], q, preferred_element_type=F32)
        m = jnp.max(s, axis=0, keepdims=True)
        acc = jnp.dot(vc_ref[0, 0], jnp.exp2(s - m).astype(BF16), preferred_element_type=F32)

        def key_step(j, carry):
            m, acc = carry
            k = kl_ref[0, pl.ds(pl.multiple_of(j * TOK_TILE, TOK_TILE), TOK_TILE), :]
            s = jnp.dot(k, q, preferred_element_type=F32)
            m_new = jnp.maximum(m, jnp.max(s, axis=0, keepdims=True))
            pv = jnp.dot(vl_ref[0, j], jnp.exp2(s - m_new).astype(BF16),
                         preferred_element_type=F32)
            return m_new, jnp.exp2(m - m_new) * acc + pv

        _, acc = lax.fori_loop(0, N_LAT_TILES, key_step, (m, acc))
        finish(qi, acc)

    def query_tiles(i, carry):
        jump = fast_tile(i * Q_UNROLL)
        for u in range(1, Q_UNROLL):
            jump = jnp.maximum(jump, fast_tile(i * Q_UNROLL + u))

        @pl.when(jnp.max(jump) > MAX_JUMP)
        def _():
            for u in range(Q_UNROLL):
                exact_tile(i * Q_UNROLL + u)

        return carry

    s_first_ref[0] = scores(0, 0)
    s_first_ref[1] = scores(0, 1)
    lax.fori_loop(0, n_q // Q_UNROLL, query_tiles, 0)


def _attention(qT, kc, kl, vcT, vlT, gT, *, group, gate_head0, name):
    n_heads = qT.shape[1]
    dv = A_V
    kv_head = lambda b, h: (b, 0, h // group)
    v_head = lambda b, h: (b, 0, h // group, 0)
    return pl.pallas_call(
        _attn_kernel,
        grid=(BATCH, n_heads),
        in_specs=[
            pl.BlockSpec((1, 1, N_Q_TILES, HEAD_PAD, Q_TILE), lambda b, h: (b, h, 0, 0, 0)),
            pl.BlockSpec((1, CTX_LEN, HEAD_PAD), kv_head),
            pl.BlockSpec((1, SEQ, HEAD_PAD), kv_head),
            pl.BlockSpec((1, 1, V_ROWS, TOK_TILE), v_head),
            pl.BlockSpec((1, N_LAT_TILES, V_ROWS, TOK_TILE), v_head),
            pl.BlockSpec((1, 1, N_Q_TILES, dv, Q_TILE), lambda b, h: (b, h + gate_head0, 0, 0, 0)),
        ],
        out_specs=pl.BlockSpec((1, 1, N_Q_TILES, dv, Q_TILE), lambda b, h: (b, h, 0, 0, 0)),
        out_shape=jax.ShapeDtypeStruct((BATCH, n_heads, N_Q_TILES, dv, Q_TILE), BF16),
        scratch_shapes=[pltpu.VMEM((2, TOK_TILE, Q_TILE), F32)],
        compiler_params=pltpu.CompilerParams(
            dimension_semantics=("arbitrary", "arbitrary"), vmem_limit_bytes=VMEM_LIMIT),
        name=name,
    )(qT, kc, kl, vcT, vlT, gT)


def _post_kernel(ma_ref, mb_ref, x_ref, mod_e_ref, mod_o_ref, woe_ref, lnge_ref, lnbe_ref,
                 wino_ref, vlng_ref, vlnb_ref, ws_ref, bs_ref, woo_ref, lngo_ref, lnbo_ref,
                 out_ref, z_ref):
    b = pl.program_id(0)
    tn = (((0,), (0,)), ((), ()))
    gate_e = mod_e_ref[pl.ds(b, 1), :][:, 2 * D_MODEL:]
    mo = mod_o_ref[pl.ds(b, 1), :]
    shift, scale, gate = mo[:, :D_MODEL], mo[:, D_MODEL:2 * D_MODEL], mo[:, 2 * D_MODEL:]

    def sub_tile(sub):
        tok = slice(sub * POST_SUB, (sub + 1) * POST_SUB)
        ma = ma_ref[0, :, 0, :, tok].reshape(A_WIDTH, POST_SUB)
        mb = mb_ref[0, :, 0, :, tok].reshape(B_WIDTH, POST_SUB)
        y = (lax.dot_general(ma, woe_ref[0:A_WIDTH, :], tn, preferred_element_type=F32)
             + lax.dot_general(mb, woe_ref[A_WIDTH:, :], tn, preferred_element_type=F32))
        yield
        x1 = _layer_norm_rows(DEEPNORM_ALPHA * x_ref[0, tok, :] + gate_e * y,
                              lnge_ref[...], lnbe_ref[...])
        xm = (x1 * (1.0 + scale) + shift).astype(BF16)
        yield
        h_u = jnp.dot(xm, wino_ref[:, :C_WIDTH], preferred_element_type=F32)
        yield
        h_v = jnp.dot(xm, wino_ref[:, C_WIDTH:2 * C_WIDTH], preferred_element_type=F32)
        u = jax.nn.gelu(h_u)
        yield
        h_g = jnp.dot(xm, wino_ref[:, 2 * C_WIDTH:], preferred_element_type=F32)
        v = _layer_norm_rows(jax.nn.gelu(h_v), vlng_ref[...], vlnb_ref[...]).astype(BF16)
        yield
        ug = u * jax.nn.silu(h_g)
        yield
        for n in range(POST_SUB // C_CHUNK):
            rows = slice(n * C_CHUNK, (n + 1) * C_CHUNK)
            zrows = slice(sub * POST_SUB + n * C_CHUNK, sub * POST_SUB + (n + 1) * C_CHUNK)
            for gi in range(C_GROUPS):
                cols = slice(gi * C_GROUP_W, (gi + 1) * C_GROUP_W)
                mixed = jnp.dot(ws_ref[gi], v[rows, cols], preferred_element_type=F32) + bs_ref[gi]
                z_ref[zrows, cols] = (ug[rows, cols] * mixed).astype(BF16)
        y2 = jnp.dot(z_ref[tok, :], woo_ref[...], preferred_element_type=F32)
        yield
        out_ref[0, tok, :] = _layer_norm_rows(DEEPNORM_ALPHA * x1 + gate * y2,
                                              lngo_ref[...], lnbo_ref[...])

    _run_staggered(sub_tile(s) for s in range(x_ref.shape[1] // POST_SUB))


def _post(ma, mb, x, mod_e, mod_o, woe, lnge, lnbe, wino, vlng, vlnb, ws, bs, woo, lngo, lnbo):
    tt = POST_TILE
    assert tt == Q_TILE
    full2 = lambda b, t: (0, 0)
    full3 = lambda b, t: (0, 0, 0)
    in_specs = [
        pl.BlockSpec((1, A_HEADS, 1, A_V, tt), lambda b, t: (b, 0, t, 0, 0)),
        pl.BlockSpec((1, B_HEADS, 1, B_HD, tt), lambda b, t: (b, 0, t, 0, 0)),
        pl.BlockSpec((1, tt, D_MODEL), lambda b, t: (b, t, 0)),
        pl.BlockSpec(mod_e.shape, full2),
        pl.BlockSpec(mod_o.shape, full2),
        pl.BlockSpec(woe.shape, full2),
        pl.BlockSpec(lnge.shape, full2),
        pl.BlockSpec(lnbe.shape, full2),
        pl.BlockSpec(wino.shape, full2),
        pl.BlockSpec(vlng.shape, full2),
        pl.BlockSpec(vlnb.shape, full2),
        pl.BlockSpec(ws.shape, full3),
        pl.BlockSpec(bs.shape, full3),
        pl.BlockSpec(woo.shape, full2),
        pl.BlockSpec(lngo.shape, full2),
        pl.BlockSpec(lnbo.shape, full2),
    ]
    return pl.pallas_call(
        _post_kernel,
        grid=(BATCH, SEQ // tt),
        in_specs=in_specs,
        out_specs=pl.BlockSpec((1, tt, D_MODEL), lambda b, t: (b, t, 0)),
        out_shape=jax.ShapeDtypeStruct((BATCH, SEQ, D_MODEL), F32),
        scratch_shapes=[pltpu.VMEM((tt, C_WIDTH), BF16)],
        compiler_params=pltpu.CompilerParams(
            dimension_semantics=("arbitrary", "arbitrary"), vmem_limit_bytes=VMEM_LIMIT),
        name="out_proj_odd_layer",
    )(ma, mb, x, mod_e, mod_o, woe, lnge, lnbe, wino, vlng, vlnb, ws, bs, woo, lngo, lnbo)


def _rope_table(d_rot):
    rows = SEQ // GRID_W
    row = jnp.repeat(jnp.arange(rows, dtype=F32), GRID_W)
    col = jnp.tile(jnp.arange(GRID_W, dtype=F32), rows)
    d_axis = d_rot // 2
    inv_freq = ROPE_THETA ** (-jnp.arange(0, d_axis, 2, dtype=F32) / d_axis)
    ang_r = row[None, :] * inv_freq[:, None]
    ang_c = col[None, :] * inv_freq[:, None]
    return jnp.concatenate([jnp.cos(ang_r), jnp.sin(ang_r), jnp.cos(ang_c), jnp.sin(ang_c)], axis=0)


def kernel(x, c, ctx, c_ctx, e_w_mod, e_b_mod, e_w_in, e_a_q_norm, e_a_kv_norm, e_a_w_uq, e_a_w_ukv, e_b_q_norm, e_b_k_norm, e_w_out, e_ln_g, e_ln_b, o_w_mod, o_b_mod, o_w_in, o_v_ln_g, o_v_ln_b, o_w_s, o_b_s, o_w_out, o_ln_g, o_ln_b):
    cond = jnp.concatenate(
        [c, c_ctx[None, :], jnp.zeros((MOD_ROWS - BATCH - 1, D_MODEL), F32)], axis=0)
    mod_e = _modulation(cond, e_w_mod[0], e_b_mod[0])
    mod_o = _modulation(cond, o_w_mod[0], o_b_mod[0])

    winT = e_w_in[0].T.astype(BF16)
    wuqT = e_a_w_uq[0].T.astype(BF16)
    wukv = e_a_w_ukv[0].reshape(A_KV_RANK, A_HEADS, A_NOPE + A_V)
    wukT = wukv[:, :, :A_NOPE].reshape(A_KV_RANK, A_HEADS * A_NOPE).T.astype(BF16)
    wuvT = wukv[:, :, A_NOPE:].reshape(A_KV_RANK, A_HEADS * A_V).T.astype(BF16)
    col = lambda v: v.reshape(-1, 1)
    row = lambda v: v.reshape(1, -1)
    proj_params = (mod_e, winT, col(e_a_q_norm[0]), col(e_a_kv_norm[0]), wuqT, wukT, wuvT,
                   col(e_b_q_norm[0]), col(e_b_k_norm[0]), _rope_table(A_ROPE), _rope_table(B_HD))

    kac, vac, kbc, vbc = _even_projection(ctx, *proj_params, latent=False)
    kal, val, kbl, vbl, qa, qb, gates = _even_projection(x, *proj_params, latent=True)

    ma = _attention(qa, kac, kal, vac, val, gates, group=1, gate_head0=0, name="attention_mla")
    mb = _attention(qb, kbc, kbl, vbc, vbl, gates, group=B_GROUP, gate_head0=A_HEADS,
                    name="attention_gqa")

    bs = jnp.broadcast_to(o_b_s[0][:, :, None], (C_GROUPS, C_CHUNK, C_GROUP_W))
    return _post(ma, mb, x, mod_e, mod_o, e_w_out[0].astype(BF16), row(e_ln_g[0]), row(e_ln_b[0]),
                 o_w_in[0].astype(BF16), row(o_v_ln_g[0]), row(o_v_ln_b[0]),
                 o_w_s[0].astype(BF16), bs, o_w_out[0].astype(BF16), row(o_ln_g[0]), row(o_ln_b[0]))
```
